```python
import jax, jax.numpy as jnp
from jax import lax
import numpy as np

D_MODEL = 1024
BATCH = 8
SEQ = 2048
DEPTH = 1

CHUNK = 64
PLE_DIM = 256
D_FF = 2816
EPS = 1e-6
HG_HEADS = 8
HG_DK = 128
HG_DV = D_MODEL // HG_HEADS
HG_KDIM = HG_HEADS * HG_DK
HG_VDIM = HG_HEADS * HG_DV
HG_BLOCK = 16
M_DINNER = 2 * D_MODEL
M_HEADDIM = 64
M_HEADS = M_DINNER // M_HEADDIM
M_STATE = 128
M_GROUPS = 4
M_CONV = 4
M_CONV_DIM = M_DINNER + 2 * M_GROUPS * M_STATE
N_BRANCH = 2

kernel_name = 'hgrn2_mamba2_gated_macaron_block'


def rms_norm(x, g):
    xf = x.astype(jnp.float32)
    y = xf * lax.rsqrt(jnp.mean(xf * xf, axis=-1, keepdims=True) + EPS)
    return (y * g.astype(jnp.float32)).astype(x.dtype)


def swiglu(x, w13, w2):
    gate, up = jnp.split(x @ w13, 2, axis=-1)
    return (jax.nn.silu(gate) * up) @ w2


def causal_depthwise_conv(x, w, b):
    c = x.shape[-1]
    k = w.shape[0]
    y = lax.conv_general_dilated(x, w[:, None, :].astype(x.dtype), window_strides=(1,),
                                 padding=[(k - 1, 0)], dimension_numbers=('NWC', 'WIO', 'NWC'),
                                 feature_group_count=c)
    return y + b.astype(x.dtype)


def hgrn2_recurrence(q, f_raw, v, lb):
    f32 = jnp.float32
    b_, s_ = q.shape[:2]
    nb = s_ // HG_BLOCK
    f = lb + (1.0 - lb) * jax.nn.sigmoid(f_raw.astype(f32))
    k = 1.0 - f

    def blocks(t, d):
        return t.astype(f32).reshape(b_, nb, HG_BLOCK, HG_HEADS, d)

    q = blocks(q, HG_DK) * (HG_DK ** -0.5)
    k = blocks(k, HG_DK)
    v = blocks(v, HG_DV)
    g_cum = jnp.cumsum(blocks(jnp.log(f), HG_DK), axis=2)
    g_last = g_cum[:, :, -1]
    q_dec = q * jnp.exp(g_cum)
    k_inv = k * jnp.exp(-g_cum)
    k_end = k * jnp.exp(g_last[:, :, None] - g_cum)
    causal = jnp.tril(jnp.ones((HG_BLOCK, HG_BLOCK), dtype=bool))
    att = jnp.einsum('bnthd,bnshd->bnhts', q_dec, k_inv)
    att = jnp.where(causal, att, 0.0)
    o_intra = jnp.einsum('bnhts,bnshv->bnthv', att, v)

    def step(state, blk):
        q_b, k_b, v_b, decay_b = blk
        o_b = jnp.einsum('bthd,bhdv->bthv', q_b, state)
        state = state * decay_b[..., None] + jnp.einsum('bshd,bshv->bhdv', k_b, v_b)
        return state, o_b

    state0 = jnp.zeros((b_, HG_HEADS, HG_DK, HG_DV), f32)
    xs = (jnp.moveaxis(q_dec, 1, 0), jnp.moveaxis(k_end, 1, 0),
          jnp.moveaxis(v, 1, 0), jnp.moveaxis(jnp.exp(g_last), 1, 0))
    _, o_inter = lax.scan(step, state0, xs)
    o = o_intra + jnp.moveaxis(o_inter, 0, 1)
    return o.reshape(b_, s_, HG_HEADS, HG_DV)


def ssd_scan(xh, dt, a_neg, b_in, c_in):
    f32 = jnp.float32
    b_, s_ = xh.shape[:2]
    nc = s_ // CHUNK
    r = M_HEADS // M_GROUPS
    x = (xh.astype(f32) * dt[..., None]).reshape(b_, nc, CHUNK, M_GROUPS, r, M_HEADDIM)
    a = (dt * a_neg).reshape(b_, nc, CHUNK, M_GROUPS, r)
    bc = b_in.astype(f32).reshape(b_, nc, CHUNK, M_GROUPS, M_STATE)
    cc = c_in.astype(f32).reshape(b_, nc, CHUNK, M_GROUPS, M_STATE)
    acs = jnp.cumsum(a, axis=2)
    causal = jnp.tril(jnp.ones((CHUNK, CHUNK), dtype=bool))
    seg = acs[:, :, :, None] - acs[:, :, None, :]
    decay = jnp.exp(jnp.where(causal[:, :, None, None], seg, -jnp.inf))
    cb = jnp.einsum('bctgn,bcsgn->bctsg', cc, bc)
    y_intra = jnp.einsum('bctsgr,bcsgrp->bctgrp', decay * cb[..., None], x)
    acs_last = acs[:, :, -1]
    w_end = jnp.exp(acs_last[:, :, None] - acs)
    w_start = jnp.exp(acs)

    def step(state, blk):
        c_b, b_b, x_b, ws_b, we_b, dl_b = blk
        y_b = jnp.einsum('btgn,bgrpn->btgrp', c_b, state) * ws_b[..., None]
        state = state * dl_b[..., None, None] + jnp.einsum('bsgn,bsgr,bsgrp->bgrpn', b_b, we_b, x_b)
        return state, y_b

    state0 = jnp.zeros((b_, M_GROUPS, r, M_HEADDIM, M_STATE), f32)
    xs = tuple(jnp.moveaxis(t, 1, 0) for t in (cc, bc, x, w_start, w_end, jnp.exp(acs_last)))
    _, y_inter = lax.scan(step, state0, xs)
    y = y_intra + jnp.moveaxis(y_inter, 0, 1)
    return y.reshape(b_, s_, M_HEADS, M_HEADDIM)


def setup_inputs(seed: int = 0) -> dict:
    key = jax.random.key(seed)
    ks = jax.random.split(key, 32)
    f32 = jnp.float32
    d_in = 2 * HG_KDIM + 2 * HG_VDIM + M_DINNER + M_CONV_DIM + M_HEADS + N_BRANCH * D_MODEL

    def nrm(k, shape, fan_in):
        return jax.random.normal(k, shape, f32) * (fan_in ** -0.5)

    def gain(k, shape):
        return 1.0 + 0.02 * jax.random.normal(k, shape, f32)

    dt0 = jnp.exp(jax.random.uniform(ks[9], (DEPTH, M_HEADS), f32, np.log(1e-3), np.log(1e-1)))
    return {
        'x': jax.random.normal(ks[0], (BATCH, SEQ, D_MODEL), f32),
        'p': jax.random.normal(ks[1], (DEPTH, BATCH, SEQ, PLE_DIM), f32),
        'ffn1_norm': gain(ks[2], (DEPTH, D_MODEL)),
        'ffn1_w13': nrm(ks[3], (DEPTH, D_MODEL, 2 * D_FF), D_MODEL),
        'ffn1_w2': nrm(ks[4], (DEPTH, D_FF, D_MODEL), D_FF),
        'mix_norm': gain(ks[5], (DEPTH, D_MODEL)),
        'w_in': nrm(ks[6], (DEPTH, D_MODEL, d_in), D_MODEL),
        'conv_w': 0.5 * jax.random.normal(ks[7], (DEPTH, M_CONV, M_CONV_DIM), f32),
        'conv_b': 0.02 * jax.random.normal(ks[8], (DEPTH, M_CONV_DIM), f32),
        'dt_bias': dt0 + jnp.log(-jnp.expm1(-dt0)),
        'a_log': jnp.log(jax.random.uniform(ks[10], (DEPTH, M_HEADS), f32, 1.0, 16.0)),
        'd_skip': gain(ks[11], (DEPTH, M_HEADS)),
        'ssm_norm': gain(ks[12], (DEPTH, M_DINNER)),
        'hg_lb': 0.1 * jax.random.normal(ks[13], (DEPTH + 1, HG_KDIM), f32),
        'hg_norm': gain(ks[14], (DEPTH, HG_VDIM)),
        'w_hg_out': nrm(ks[15], (DEPTH, HG_VDIM, D_MODEL), HG_VDIM),
        'w_ssm_out': nrm(ks[16], (DEPTH, M_DINNER, D_MODEL), M_DINNER),
        'w_out': nrm(ks[17], (DEPTH, D_MODEL, D_MODEL), D_MODEL),
        'ffn2_norm': gain(ks[18], (DEPTH, D_MODEL)),
        'ffn2_w13': nrm(ks[19], (DEPTH, D_MODEL, 2 * D_FF), D_MODEL),
        'ffn2_w2': nrm(ks[20], (DEPTH, D_FF, D_MODEL), D_FF),
        'ple_norm': gain(ks[21], (DEPTH, D_MODEL)),
        'w_ple_gate': nrm(ks[22], (DEPTH, D_MODEL, D_MODEL), D_MODEL),
        'w_ple_proj': nrm(ks[23], (DEPTH, PLE_DIM, D_MODEL), PLE_DIM),
        'final_norm': gain(ks[24], (D_MODEL,)),
    }


def reference(x, p, ffn1_norm, ffn1_w13, ffn1_w2, mix_norm, w_in, conv_w, conv_b, dt_bias,
              a_log, d_skip, ssm_norm, hg_lb, hg_norm, w_hg_out, w_ssm_out, w_out,
              ffn2_norm, ffn2_w13, ffn2_w2, ple_norm, w_ple_gate, w_ple_proj, final_norm):
    f32 = jnp.float32
    b_, s_, _ = x.shape
    sizes = (HG_KDIM, HG_KDIM, HG_VDIM, HG_VDIM, M_DINNER, M_CONV_DIM, M_HEADS, N_BRANCH * D_MODEL)
    split_at = np.cumsum(sizes)[:-1].tolist()
    lower_bounds = jnp.cumsum(jax.nn.softmax(hg_lb.astype(f32), axis=0), axis=0)
    h = x
    for l in range(DEPTH):
        h = h + 0.5 * swiglu(rms_norm(h, ffn1_norm[l]), ffn1_w13[l], ffn1_w2[l])
        n = rms_norm(h, mix_norm[l])
        hq, hf, hi, hgate, mz, mxbc, mdt, br_gates = jnp.split(n @ w_in[l], split_at, axis=-1)
        o_hg = hgrn2_recurrence(hq, hf, hi, lower_bounds[l])
        o_hg = rms_norm(o_hg, hg_norm[l].reshape(HG_HEADS, HG_DV)).reshape(b_, s_, HG_VDIM)
        o_hg = (o_hg * jax.nn.silu(hgate.astype(f32))).astype(h.dtype)
        xbc = jax.nn.silu(causal_depthwise_conv(mxbc, conv_w[l], conv_b[l]))
        xs, bm, cm = jnp.split(xbc, [M_DINNER, M_DINNER + M_GROUPS * M_STATE], axis=-1)
        xh = xs.reshape(b_, s_, M_HEADS, M_HEADDIM)
        dt = jax.nn.softplus(mdt.astype(f32) + dt_bias[l].astype(f32))
        a_neg = -jnp.exp(a_log[l].astype(f32))
        y = ssd_scan(xh, dt, a_neg, bm.reshape(b_, s_, M_GROUPS, M_STATE),
                     cm.reshape(b_, s_, M_GROUPS, M_STATE))
        y = y + d_skip[l].astype(f32)[:, None] * xh.astype(f32)
        y = y.reshape(b_, s_, M_DINNER) * jax.nn.silu(mz.astype(f32))
        y = rms_norm(y.reshape(b_, s_, M_GROUPS, M_DINNER // M_GROUPS),
                     ssm_norm[l].reshape(M_GROUPS, M_DINNER // M_GROUPS))
        y = y.reshape(b_, s_, M_DINNER).astype(h.dtype)
        gate_a, gate_b = jnp.split(jax.nn.sigmoid(br_gates), 2, axis=-1)
        mixed = gate_a * (o_hg @ w_hg_out[l]) + gate_b * (y @ w_ssm_out[l])
        h = h + mixed @ w_out[l]
        h = h + 0.5 * swiglu(rms_norm(h, ffn2_norm[l]), ffn2_w13[l], ffn2_w2[l])
        ple_gate = jax.nn.sigmoid(rms_norm(h, ple_norm[l]) @ w_ple_gate[l])
        h = h + ple_gate * (p[l].astype(h.dtype) @ w_ple_proj[l])
    return rms_norm(h, final_norm)
```

```python
import functools

import jax
import jax.numpy as jnp
from jax import lax
from jax.experimental import pallas as pl
from jax.experimental.pallas import tpu as pltpu

F32 = jnp.float32
BF16 = jnp.bfloat16

D_MODEL = 1024
D_FF = 2816
PLE_DIM = 256
EPS = 1e-6
HG_HEADS = 8
HG_DK = 128
HG_DV = 128
HG_KDIM = HG_HEADS * HG_DK
HG_VDIM = HG_HEADS * HG_DV
HG_BLOCK = 16
M_DINNER = 2048
M_HEADDIM = 64
M_HEADS = 32
M_STATE = 128
M_GROUPS = 4
M_CONV = 4
M_GROUP_W = M_DINNER // M_GROUPS
M_CONV_DIM = M_DINNER + 2 * M_GROUPS * M_STATE
CHUNK = 64

ROW_TILE = 512
SEQ_TILE = 256
FF_CHUNK = 1408
VMEM_LIMIT = 56 * 1024 * 1024


def _dot(a, b):
    return jnp.dot(a, b, preferred_element_type=F32)


def _dot_nt(a, b):
    return lax.dot_general(a, b, (((1,), (1,)), ((), ())), preferred_element_type=F32)


def _dot_tn(a, b):
    return lax.dot_general(a, b, (((0,), (0,)), ((), ())), preferred_element_type=F32)


def _split_bf16(x, parts):
    out, rem = [], x
    for _ in range(parts):
        p = rem.astype(BF16)
        out.append(p)
        rem = rem - p.astype(F32)
    return out


def _dot_sel(sel, x, parts=3):
    acc = None
    for p in _split_bf16(x, parts):
        t = _dot(sel, p)
        acc = t if acc is None else acc + t
    return acc


def _dot_sel_r(x, sel, parts=3):
    acc = None
    for p in _split_bf16(x, parts):
        t = _dot(p, sel)
        acc = t if acc is None else acc + t
    return acc


def _rms(x, g):
    ms = jnp.mean(x * x, axis=-1, keepdims=True)
    return x * lax.rsqrt(ms + EPS) * g


def _silu(x):
    return x * jax.nn.sigmoid(x)


def _softplus(x):
    return jnp.maximum(x, 0.0) + jnp.log(1.0 + jnp.exp(-jnp.abs(x)))


def _resident(shape):
    nd = len(shape)
    return pl.BlockSpec(shape, lambda *_: (0,) * nd, pipeline_mode=pl.Buffered(1))


def _swiglu_residual(x, g_ref, w13_ref, w2_ref):
    n = _rms(x, g_ref[...]).astype(BF16)
    acc = jnp.zeros(x.shape, F32)
    for c in range(D_FF // FF_CHUNK):
        lo = c * FF_CHUNK
        gate = _dot(n, w13_ref[:, lo:lo + FF_CHUNK])
        up = _dot(n, w13_ref[:, D_FF + lo:D_FF + lo + FF_CHUNK])
        act = (_silu(gate) * up).astype(BF16)
        acc = acc + _dot(act, w2_ref[lo:lo + FF_CHUNK, :])
    return x + 0.5 * acc


def _ffn1_kernel(x_ref, g_ref, w13_ref, w2_ref, mixg_ref, h_ref, n_ref):
    h = _swiglu_residual(x_ref[...], g_ref, w13_ref, w2_ref)
    h_ref[...] = h
    n_ref[...] = _rms(h, mixg_ref[...]).astype(BF16)


def _ffn2_kernel(x_ref, g_ref, w13_ref, w2_ref, p_ref, pleg_ref, wpg_ref, wpp_ref, fing_ref, o_ref):
    h = _swiglu_residual(x_ref[...], g_ref, w13_ref, w2_ref)
    gate = jax.nn.sigmoid(_dot(_rms(h, pleg_ref[...]).astype(BF16), wpg_ref[...]))
    emb = _dot(p_ref[...].astype(BF16), wpp_ref[...])
    h = h + gate * emb
    o_ref[...] = _rms(h, fing_ref[...])


def _row_spec(width):
    return pl.BlockSpec((ROW_TILE, width), lambda i: (i, 0))


def _ffn1(x, g, w13, w2, mixg):
    t = x.shape[0]
    return pl.pallas_call(
        _ffn1_kernel,
        grid=(t // ROW_TILE,),
        in_specs=[_row_spec(D_MODEL), _resident((1, D_MODEL)), _resident(w13.shape),
                  _resident(w2.shape), _resident((1, D_MODEL))],
        out_specs=[_row_spec(D_MODEL), _row_spec(D_MODEL)],
        out_shape=[jax.ShapeDtypeStruct((t, D_MODEL), F32),
                   jax.ShapeDtypeStruct((t, D_MODEL), BF16)],
        compiler_params=pltpu.CompilerParams(dimension_semantics=("arbitrary",),
                                             vmem_limit_bytes=VMEM_LIMIT),
        name="ffn1",
    )(x, g, w13, w2, mixg)


def _ffn2(x, g, w13, w2, p, pleg, wpg, wpp, fing):
    t = x.shape[0]
    return pl.pallas_call(
        _ffn2_kernel,
        grid=(t // ROW_TILE,),
        in_specs=[_row_spec(D_MODEL), _resident((1, D_MODEL)), _resident(w13.shape),
                  _resident(w2.shape), _row_spec(PLE_DIM), _resident((1, D_MODEL)),
                  _resident(wpg.shape), _resident(wpp.shape), _resident((1, D_MODEL))],
        out_specs=_row_spec(D_MODEL),
        out_shape=jax.ShapeDtypeStruct((t, D_MODEL), F32),
        compiler_params=pltpu.CompilerParams(dimension_semantics=("arbitrary",),
                                             vmem_limit_bytes=VMEM_LIMIT),
        name="ffn2",
    )(x, g, w13, w2, p, pleg, wpg, wpp, fing)


def _bcast_rows(row, rows):
    return jnp.broadcast_to(row, (rows, row.shape[-1]))


def _hgrn_kernel(n_ref, w_ref, lb_ref, gn_ref, o_ref,
                 st_ref, q_ref, k_ref, gl_ref, v_ref, oacc_ref):
    @pl.when(pl.program_id(1) == 0)
    def _():
        st_ref[...] = jnp.zeros_like(st_ref)

    n = n_ref[...]
    a = lb_ref[...]
    e = jnp.exp(a - jnp.max(a, axis=0, keepdims=True))
    lb = e[0:1] / jnp.sum(e, axis=0, keepdims=True)

    f = lb + (1.0 - lb) * jax.nn.sigmoid(_dot(n, w_ref[:, HG_KDIM:2 * HG_KDIM]))
    k_ref[...] = 1.0 - f
    r = lax.broadcasted_iota(jnp.int32, (SEQ_TILE, SEQ_TILE), 0)
    c = lax.broadcasted_iota(jnp.int32, (SEQ_TILE, SEQ_TILE), 1)
    tril_blk = ((c <= r) & (c // HG_BLOCK == r // HG_BLOCK)).astype(BF16)
    gl_ref[...] = _dot_sel(tril_blk, jnp.log(f))
    q_ref[...] = _dot(n, w_ref[:, 0:HG_KDIM]) * (HG_DK ** -0.5)
    v_ref[...] = _dot(n, w_ref[:, 2 * HG_KDIM:2 * HG_KDIM + HG_VDIM]).astype(BF16)

    t_i = lax.broadcasted_iota(jnp.int32, (CHUNK, CHUNK), 0)
    s_i = lax.broadcasted_iota(jnp.int32, (CHUNK, CHUNK), 1)
    bt, bs = t_i // HG_BLOCK, s_i // HG_BLOCK
    m_same = (bt == bs) & (s_i <= t_i)
    m_adj = (bt % 2 == 1) & (bs == bt - 1)
    m_half = (bt >= 2) & (bs < 2)
    ones_blk = jnp.ones((HG_BLOCK, HG_KDIM), F32)

    def chunk_step(ci, carry):
        r0 = pl.multiple_of(ci * CHUNK, CHUNK)
        rows = pl.ds(r0, CHUNK)
        q, k, gl, v = q_ref[rows, :], k_ref[rows, :], gl_ref[rows, :], v_ref[rows, :]
        tot = [gl_ref[pl.ds(r0 + HG_BLOCK * i + HG_BLOCK - 1, 1), :] for i in range(4)]
        e_t = [jnp.exp(t) for t in tot]
        e01, e23 = e_t[0] * e_t[1], e_t[2] * e_t[3]
        dec = e01 * e23
        tot_rows = jnp.concatenate([_bcast_rows(t, HG_BLOCK) for t in tot], axis=0)
        qd = q * jnp.exp(gl)
        ki = k * jnp.exp(-gl)
        ke = k * jnp.exp(tot_rows - gl)
        qd32 = qd * jnp.concatenate(
            [ones_blk, _bcast_rows(e_t[0], HG_BLOCK), ones_blk, _bcast_rows(e_t[2], HG_BLOCK)], axis=0)
        ke32 = ke * jnp.concatenate(
            [_bcast_rows(e_t[1], HG_BLOCK), ones_blk, _bcast_rows(e_t[3], HG_BLOCK), ones_blk], axis=0)
        qd64 = qd32 * jnp.concatenate(
            [ones_blk, ones_blk, _bcast_rows(e01, HG_BLOCK), _bcast_rows(e01, HG_BLOCK)], axis=0)
        ke64 = ke32 * jnp.concatenate(
            [_bcast_rows(e23, HG_BLOCK), _bcast_rows(e23, HG_BLOCK), ones_blk, ones_blk], axis=0)
        qd, ki, ke, qd32, ke32, qd64, ke64 = (
            t.astype(BF16) for t in (qd, ki, ke, qd32, ke32, qd64, ke64))
        for h in range(HG_HEADS):
            kl = slice(h * HG_DK, (h + 1) * HG_DK)
            vl = slice(h * HG_DV, (h + 1) * HG_DV)
            a_same = _dot_nt(qd[:, kl], ki[:, kl])
            a_adj = _dot_nt(qd[:, kl], ke[:, kl])
            a_half = _dot_nt(qd32[:, kl], ke32[:, kl])
            att = jnp.where(m_same, a_same, jnp.where(m_adj, a_adj, jnp.where(m_half, a_half, 0.0)))
            st = st_ref[h]
            o = _dot(att.astype(BF16), v[:, vl]) + _dot_nt(qd64[:, kl], st.astype(BF16))
            oacc_ref[rows, vl] = o
            st_ref[h] = st * dec[:, kl] + _dot_tn(v[:, vl], ke64[:, kl])
        return carry

    lax.fori_loop(0, SEQ_TILE // CHUNK, chunk_step, 0)

    gate = _dot(n, w_ref[:, 2 * HG_KDIM + HG_VDIM:2 * HG_KDIM + 2 * HG_VDIM])
    gn = gn_ref[...]
    for h in range(HG_HEADS):
        vl = slice(h * HG_DV, (h + 1) * HG_DV)
        o_ref[:, vl] = (_rms(oacc_ref[:, vl], gn[:, vl]) * _silu(gate[:, vl])).astype(BF16)


def _hgrn(n, w, hg_lb, hg_norm, batch, seq):
    tiles = seq // SEQ_TILE
    seq_spec = pl.BlockSpec((SEQ_TILE, D_MODEL), lambda b, j: (b * tiles + j, 0))
    return pl.pallas_call(
        _hgrn_kernel,
        grid=(batch, tiles),
        in_specs=[seq_spec, _resident(w.shape), _resident(hg_lb.shape), _resident((1, HG_VDIM))],
        out_specs=pl.BlockSpec((SEQ_TILE, HG_VDIM), lambda b, j: (b * tiles + j, 0)),
        out_shape=jax.ShapeDtypeStruct((batch * seq, HG_VDIM), BF16),
        scratch_shapes=[
            pltpu.VMEM((HG_HEADS, HG_DV, HG_DK), F32),
            pltpu.VMEM((SEQ_TILE, HG_KDIM), F32),
            pltpu.VMEM((SEQ_TILE, HG_KDIM), F32),
            pltpu.VMEM((SEQ_TILE, HG_KDIM), F32),
            pltpu.VMEM((SEQ_TILE, HG_VDIM), BF16),
            pltpu.VMEM((SEQ_TILE, HG_VDIM), F32),
        ],
        compiler_params=pltpu.CompilerParams(dimension_semantics=("arbitrary", "arbitrary"),
                                             vmem_limit_bytes=VMEM_LIMIT),
        name="hgrn2",
    )(n, w, hg_lb, hg_norm)


_PAIRS = M_HEADS // 2
_PAIR_W = 2 * M_HEADDIM


def _ssd_kernel(n_ref, wz_ref, wxbc_ref, wdt_ref, wdte_ref, wdto_ref,
                cw_ref, cb_ref, dtb_ref, aneg_ref, dtbe_ref, dtbo_ref, anege_ref, anego_ref,
                dsk_ref, gn_ref, y_ref,
                st_ref, buf_ref, xbc_ref, dt_ref, acs_ref, yacc_ref):
    first = pl.program_id(1) == 0

    @pl.when(first)
    def _():
        st_ref[...] = jnp.zeros_like(st_ref)
        buf_ref[0:8, :] = jnp.zeros((8, M_CONV_DIM), F32)

    n = n_ref[...]
    buf_ref[8:8 + SEQ_TILE, :] = _dot(n, wxbc_ref[...])
    conv = _bcast_rows(cb_ref[...], SEQ_TILE)
    for kk in range(M_CONV):
        conv = conv + buf_ref[pl.ds(8 - (M_CONV - 1) + kk, SEQ_TILE), :] * cw_ref[kk:kk + 1, :]
    buf_ref[0:8, :] = buf_ref[SEQ_TILE:SEQ_TILE + 8, :]
    xbc_ref[...] = _silu(conv)

    dt = _softplus(_dot(n, wdt_ref[...]) + dtb_ref[...])
    dt_ref[...] = dt
    r = lax.broadcasted_iota(jnp.int32, (SEQ_TILE, SEQ_TILE), 0)
    c = lax.broadcasted_iota(jnp.int32, (SEQ_TILE, SEQ_TILE), 1)
    tril_chunk = ((c <= r) & (c // CHUNK == r // CHUNK)).astype(BF16)
    acs_ref[...] = _dot_sel(tril_chunk, dt * aneg_ref[...])

    t_i = lax.broadcasted_iota(jnp.int32, (CHUNK, _PAIR_W), 0)
    l_i = lax.broadcasted_iota(jnp.int32, (CHUNK, _PAIR_W), 1)
    causal2 = (l_i % CHUNK) <= t_i
    left = l_i < M_HEADDIM
    u_r = lax.broadcasted_iota(jnp.int32, (_PAIR_W, _PAIR_W), 0)
    u_c = lax.broadcasted_iota(jnp.int32, (_PAIR_W, _PAIR_W), 1)
    triu2 = ((u_r <= u_c) & (u_r // CHUNK == u_c // CHUNK)).astype(BF16)
    lane_left = lax.broadcasted_iota(jnp.int32, (_PAIRS, _PAIR_W), 1) < CHUNK

    def chunk_step(ci, carry):
        r0 = pl.multiple_of(ci * CHUNK, CHUNK)
        rows = pl.ds(r0, CHUNK)
        nc = n_ref[rows, :]
        n2 = jnp.concatenate([nc, nc], axis=0)
        dte = _softplus(_dot_nt(wdte_ref[...], n2) + dtbe_ref[...]) * anege_ref[...]
        dto = _softplus(_dot_nt(wdto_ref[...], n2) + dtbo_ref[...]) * anego_ref[...]
        a_t = jnp.where(lane_left, dte, dto)
        acs_t = _dot_sel_r(a_t, triu2)
        acs = acs_ref[rows, :]
        dt_c = dt_ref[rows, :]
        acs_last = acs_ref[pl.ds(r0 + CHUNK - 1, 1), :]
        xs = xbc_ref[rows, 0:M_DINNER]
        xdt = xs * dt_c
        xw = (xdt * jnp.exp(acs_last - acs)).astype(BF16)
        w_start = jnp.exp(acs)
        dl = jnp.exp(acs_last)
        xdt = xdt.astype(BF16)
        for g in range(M_GROUPS):
            b_g = xbc_ref[rows, M_DINNER + g * M_STATE:M_DINNER + (g + 1) * M_STATE].astype(BF16)
            c_g = xbc_ref[rows, M_DINNER + (M_GROUPS + g) * M_STATE:
                          M_DINNER + (M_GROUPS + g + 1) * M_STATE].astype(BF16)
            cb2 = _dot_nt(c_g, jnp.concatenate([b_g, b_g], axis=0))
            gl = slice(g * M_GROUP_W, (g + 1) * M_GROUP_W)
            st = st_ref[g]
            y_g = _dot(c_g, st.astype(BF16)) * w_start[:, gl]
            yacc_ref[rows, gl] = y_g
            st_ref[g] = st * dl[:, gl] + _dot_tn(b_g, xw[:, gl])
            for pp in range(M_GROUP_W // _PAIR_W):
                p = g * (M_GROUP_W // _PAIR_W) + pp
                pl_ = slice(p * _PAIR_W, (p + 1) * _PAIR_W)
                seg = acs[:, pl_] - acs_t[p:p + 1, :]
                decay = jnp.exp(jnp.where(causal2, seg, -jnp.inf))
                m = (cb2 * decay).astype(BF16)
                xp = xdt[:, pl_]
                zero = jnp.zeros_like(xp)
                rhs = jnp.concatenate([jnp.where(left, xp, zero), jnp.where(left, zero, xp)], axis=0)
                yacc_ref[rows, pl_] += _dot(m, rhs)
        return carry

    lax.fori_loop(0, SEQ_TILE // CHUNK, chunk_step, 0)

    z = _dot(n, wz_ref[...])
    y = (yacc_ref[...] + dsk_ref[...] * xbc_ref[:, 0:M_DINNER]) * _silu(z)
    gn = gn_ref[...]
    for g in range(M_GROUPS):
        gl = slice(g * M_GROUP_W, (g + 1) * M_GROUP_W)
        y_ref[:, gl] = _rms(y[:, gl], gn[:, gl]).astype(BF16)


def _ssd(n, wz, wxbc, wdt, wdte, wdto, cw, cb, dtb, aneg, dtbe, dtbo, anege, anego, dsk, gn,
         batch, seq):
    tiles = seq // SEQ_TILE
    seq_spec = pl.BlockSpec((SEQ_TILE, D_MODEL), lambda b, j: (b * tiles + j, 0))
    params = (wz, wxbc, wdt, wdte, wdto, cw, cb, dtb, aneg, dtbe, dtbo, anege, anego, dsk, gn)
    return pl.pallas_call(
        _ssd_kernel,
        grid=(batch, tiles),
        in_specs=[seq_spec] + [_resident(a.shape) for a in params],
        out_specs=pl.BlockSpec((SEQ_TILE, M_DINNER), lambda b, j: (b * tiles + j, 0)),
        out_shape=jax.ShapeDtypeStruct((batch * seq, M_DINNER), BF16),
        scratch_shapes=[
            pltpu.VMEM((M_GROUPS, M_STATE, M_GROUP_W), F32),
            pltpu.VMEM((SEQ_TILE + 8, M_CONV_DIM), F32),
            pltpu.VMEM((SEQ_TILE, M_CONV_DIM), F32),
            pltpu.VMEM((SEQ_TILE, M_DINNER), F32),
            pltpu.VMEM((SEQ_TILE, M_DINNER), F32),
            pltpu.VMEM((SEQ_TILE, M_DINNER), F32),
        ],
        compiler_params=pltpu.CompilerParams(dimension_semantics=("arbitrary", "arbitrary"),
                                             vmem_limit_bytes=VMEM_LIMIT),
        name="ssd",
    )(n, *params)


def _merge_kernel(h_ref, n_ref, ohg_ref, y_ref, wbr_ref, whg_ref, wssm_ref, wout_ref, o_ref):
    br = jax.nn.sigmoid(_dot(n_ref[...], wbr_ref[...]))
    mixed = (br[:, 0:D_MODEL] * _dot(ohg_ref[...], whg_ref[...])
             + br[:, D_MODEL:2 * D_MODEL] * _dot(y_ref[...], wssm_ref[...]))
    o_ref[...] = h_ref[...] + _dot(mixed.astype(BF16), wout_ref[...])


def _merge(h, n, ohg, y, wbr, whg, wssm, wout):
    t = h.shape[0]
    return pl.pallas_call(
        _merge_kernel,
        grid=(t // ROW_TILE,),
        in_specs=[_row_spec(D_MODEL), _row_spec(D_MODEL), _row_spec(HG_VDIM), _row_spec(M_DINNER),
                  _resident(wbr.shape), _resident(whg.shape), _resident(wssm.shape),
                  _resident(wout.shape)],
        out_specs=_row_spec(D_MODEL),
        out_shape=jax.ShapeDtypeStruct((t, D_MODEL), F32),
        compiler_params=pltpu.CompilerParams(dimension_semantics=("arbitrary",),
                                             vmem_limit_bytes=VMEM_LIMIT),
        name="merge",
    )(h, n, ohg, y, wbr, whg, wssm, wout)


def kernel(x, p, ffn1_norm, ffn1_w13, ffn1_w2, mix_norm, w_in, conv_w, conv_b, dt_bias, a_log,
           d_skip, ssm_norm, hg_lb, hg_norm, w_hg_out, w_ssm_out, w_out, ffn2_norm, ffn2_w13,
           ffn2_w2, ple_norm, w_ple_gate, w_ple_proj, final_norm):
    batch, seq, _ = x.shape
    assert ffn1_w13.shape[0] == 1 and hg_lb.shape[0] == 2, "single-layer block"
    assert seq % SEQ_TILE == 0 and (batch * seq) % ROW_TILE == 0
    t = batch * seq
    row = lambda v: v.reshape(1, -1).astype(F32)
    col = lambda v: v.reshape(-1, 1).astype(F32)
    expand = lambda v: jnp.repeat(v, M_HEADDIM, axis=-1)

    sizes = (HG_KDIM, HG_KDIM, HG_VDIM, HG_VDIM, M_DINNER, M_CONV_DIM, M_HEADS, 2 * D_MODEL)
    offs = [0]
    for s in sizes:
        offs.append(offs[-1] + s)
    w = w_in[0]
    w_hg = w[:, offs[0]:offs[4]].astype(BF16)
    w_z = w[:, offs[4]:offs[5]].astype(BF16)
    w_xbc = w[:, offs[5]:offs[6]].astype(BF16)
    w_dt = w[:, offs[6]:offs[7]].astype(BF16)
    w_br = w[:, offs[7]:offs[8]].astype(BF16)
    a_neg = -jnp.exp(a_log[0].astype(F32))
    dtb = dt_bias[0].astype(F32)

    x2 = x.reshape(t, D_MODEL)
    h1, n = _ffn1(x2, row(ffn1_norm[0]), ffn1_w13[0].astype(BF16), ffn1_w2[0].astype(BF16),
                  row(mix_norm[0]))
    o_hg = _hgrn(n, w_hg, hg_lb.astype(F32), row(hg_norm[0]), batch, seq)
    y = _ssd(n, w_z, w_xbc, expand(w_dt), w_dt[:, 0::2].T, w_dt[:, 1::2].T,
             conv_w[0].astype(F32), row(conv_b[0]), row(expand(dtb)), row(expand(a_neg)),
             col(dtb[0::2]), col(dtb[1::2]), col(a_neg[0::2]), col(a_neg[1::2]),
             row(expand(d_skip[0].astype(F32))), row(ssm_norm[0]), batch, seq)
    h2 = _merge(h1, n, o_hg, y, w_br, w_hg_out[0].astype(BF16), w_ssm_out[0].astype(BF16),
                w_out[0].astype(BF16))
    out = _ffn2(h2, row(ffn2_norm[0]), ffn2_w13[0].astype(BF16), ffn2_w2[0].astype(BF16),
                p[0].reshape(t, PLE_DIM), row(ple_norm[0]), w_ple_gate[0].astype(BF16),
                w_ple_proj[0].astype(BF16), row(final_norm))
    return out.reshape(batch, seq, D_MODEL)
```

```python
import functools

import jax
import jax.numpy as jnp
from jax import lax
from jax.experimental import pallas as pl
from jax.experimental.pallas import tpu as pltpu

F32 = jnp.float32
BF16 = jnp.bfloat16

D_MODEL = 1024
D_FF = 2816
PLE_DIM = 256
EPS = 1e-6
HG_HEADS = 8
HG_DK = 128
HG_DV = 128
HG_KDIM = HG_HEADS * HG_DK
HG_VDIM = HG_HEADS * HG_DV
HG_BLOCK = 16
M_DINNER = 2048
M_HEADDIM = 64
M_HEADS = 32
M_STATE = 128
M_GROUPS = 4
M_CONV = 4
M_GROUP_W = M_DINNER // M_GROUPS
M_CONV_DIM = M_DINNER + 2 * M_GROUPS * M_STATE
CHUNK = 64

ROW_TILE = 512
SEQ_TILE = 256
FF_CHUNK = 1408
VMEM_LIMIT = 56 * 1024 * 1024


def _dot(a, b):
    return jnp.dot(a, b, preferred_element_type=F32)


def _dot_nt(a, b):
    return lax.dot_general(a, b, (((1,), (1,)), ((), ())), preferred_element_type=F32)


def _dot_tn(a, b):
    return lax.dot_general(a, b, (((0,), (0,)), ((), ())), preferred_element_type=F32)


def _split_bf16(x, parts):
    out, rem = [], x
    for _ in range(parts):
        p = rem.astype(BF16)
        out.append(p)
        rem = rem - p.astype(F32)
    return out


def _dot_sel(sel, x, parts=3):
    acc = None
    for p in _split_bf16(x, parts):
        t = _dot(sel, p)
        acc = t if acc is None else acc + t
    return acc


def _dot_sel_r(x, sel, parts=3):
    acc = None
    for p in _split_bf16(x, parts):
        t = _dot(p, sel)
        acc = t if acc is None else acc + t
    return acc


def _rms(x, g):
    ms = jnp.mean(x * x, axis=-1, keepdims=True)
    return x * lax.rsqrt(ms + EPS) * g


def _sigmoid(x):
    return 0.5 * jnp.tanh(0.5 * x) + 0.5


def _silu(x):
    return x * _sigmoid(x)


def _softplus(x):
    return jnp.maximum(x, 0.0) + jnp.log(1.0 + jnp.exp(-jnp.abs(x)))


def _resident(shape):
    nd = len(shape)
    return pl.BlockSpec(shape, lambda *_: (0,) * nd, pipeline_mode=pl.Buffered(1))


def _swiglu_residual(x, g_ref, w13_ref, w2_ref):
    n = _rms(x, g_ref[...]).astype(BF16)
    acc = jnp.zeros(x.shape, F32)
    for c in range(D_FF // FF_CHUNK):
        lo = c * FF_CHUNK
        gate = _dot(n, w13_ref[:, lo:lo + FF_CHUNK])
        up = _dot(n, w13_ref[:, D_FF + lo:D_FF + lo + FF_CHUNK])
        act = (_silu(gate) * up).astype(BF16)
        acc = acc + _dot(act, w2_ref[lo:lo + FF_CHUNK, :])
    return x + 0.5 * acc


def _ffn1_kernel(x_ref, g_ref, w13_ref, w2_ref, mixg_ref, h_ref, n_ref):
    h = _swiglu_residual(x_ref[...], g_ref, w13_ref, w2_ref)
    h_ref[...] = h
    n_ref[...] = _rms(h, mixg_ref[...]).astype(BF16)


def _ffn2_kernel(x_ref, g_ref, w13_ref, w2_ref, p_ref, pleg_ref, wpg_ref, wpp_ref, fing_ref, o_ref):
    h = _swiglu_residual(x_ref[...], g_ref, w13_ref, w2_ref)
    gate = _sigmoid(_dot(_rms(h, pleg_ref[...]).astype(BF16), wpg_ref[...]))
    emb = _dot(p_ref[...].astype(BF16), wpp_ref[...])
    h = h + gate * emb
    o_ref[...] = _rms(h, fing_ref[...])


def _row_spec(width):
    return pl.BlockSpec((ROW_TILE, width), lambda i: (i, 0))


def _ffn1(x, g, w13, w2, mixg):
    t = x.shape[0]
    return pl.pallas_call(
        _ffn1_kernel,
        grid=(t // ROW_TILE,),
        in_specs=[_row_spec(D_MODEL), _resident((1, D_MODEL)), _resident(w13.shape),
                  _resident(w2.shape), _resident((1, D_MODEL))],
        out_specs=[_row_spec(D_MODEL), _row_spec(D_MODEL)],
        out_shape=[jax.ShapeDtypeStruct((t, D_MODEL), F32),
                   jax.ShapeDtypeStruct((t, D_MODEL), BF16)],
        compiler_params=pltpu.CompilerParams(dimension_semantics=("arbitrary",),
                                             vmem_limit_bytes=VMEM_LIMIT),
        name="ffn1",
    )(x, g, w13, w2, mixg)


def _ffn2(x, g, w13, w2, p, pleg, wpg, wpp, fing):
    t = x.shape[0]
    return pl.pallas_call(
        _ffn2_kernel,
        grid=(t // ROW_TILE,),
        in_specs=[_row_spec(D_MODEL), _resident((1, D_MODEL)), _resident(w13.shape),
                  _resident(w2.shape), _row_spec(PLE_DIM), _resident((1, D_MODEL)),
                  _resident(wpg.shape), _resident(wpp.shape), _resident((1, D_MODEL))],
        out_specs=_row_spec(D_MODEL),
        out_shape=jax.ShapeDtypeStruct((t, D_MODEL), F32),
        compiler_params=pltpu.CompilerParams(dimension_semantics=("arbitrary",),
                                             vmem_limit_bytes=VMEM_LIMIT),
        name="ffn2",
    )(x, g, w13, w2, p, pleg, wpg, wpp, fing)


def _bcast_rows(row, rows):
    return jnp.broadcast_to(row, (rows, row.shape[-1]))


def _hgrn_kernel(n_ref, w_ref, lb_ref, gn_ref, o_ref,
                 st_ref, q_ref, k_ref, gl_ref, v_ref, oacc_ref):
    @pl.when(pl.program_id(1) == 0)
    def _():
        st_ref[...] = jnp.zeros_like(st_ref)

    n = n_ref[...]
    a = lb_ref[...]
    e = jnp.exp(a - jnp.max(a, axis=0, keepdims=True))
    lb = e[0:1] / jnp.sum(e, axis=0, keepdims=True)

    f = lb + (1.0 - lb) * _sigmoid(_dot(n, w_ref[:, HG_KDIM:2 * HG_KDIM]))
    k_ref[...] = 1.0 - f
    r = lax.broadcasted_iota(jnp.int32, (SEQ_TILE, SEQ_TILE), 0)
    c = lax.broadcasted_iota(jnp.int32, (SEQ_TILE, SEQ_TILE), 1)
    tril_blk = ((c <= r) & (c // HG_BLOCK == r // HG_BLOCK)).astype(BF16)
    gl_ref[...] = _dot_sel(tril_blk, jnp.log(f))
    q_ref[...] = _dot(n, w_ref[:, 0:HG_KDIM]) * (HG_DK ** -0.5)
    v_ref[...] = _dot(n, w_ref[:, 2 * HG_KDIM:2 * HG_KDIM + HG_VDIM]).astype(BF16)

    t_i = lax.broadcasted_iota(jnp.int32, (CHUNK, 3 * CHUNK), 0)
    c_i = lax.broadcasted_iota(jnp.int32, (CHUNK, 3 * CHUNK), 1)
    s_i, grp = c_i % CHUNK, c_i // CHUNK
    bt, bs = t_i // HG_BLOCK, s_i // HG_BLOCK
    m_all = (((grp == 0) & (bt == bs) & (s_i <= t_i))
             | ((grp == 1) & (bt % 2 == 1) & (bs == bt - 1))
             | ((grp == 2) & (bt >= 2) & (bs < 2)))
    zero_blk = jnp.zeros((CHUNK, HG_DK), BF16)
    ones_blk = jnp.ones((HG_BLOCK, HG_KDIM), F32)

    atts = {}
    prepped = []
    for ci in range(SEQ_TILE // CHUNK):
        rows = slice(ci * CHUNK, (ci + 1) * CHUNK)
        q, k, gl, v = q_ref[rows, :], k_ref[rows, :], gl_ref[rows, :], v_ref[rows, :]
        tot = [gl[HG_BLOCK * i + HG_BLOCK - 1:HG_BLOCK * (i + 1), :] for i in range(4)]
        e_t = [jnp.exp(t) for t in tot]
        e01, e23 = e_t[0] * e_t[1], e_t[2] * e_t[3]
        dec = e01 * e23
        tot_rows = jnp.concatenate([_bcast_rows(t, HG_BLOCK) for t in tot], axis=0)
        qd = q * jnp.exp(gl)
        ki = k * jnp.exp(-gl)
        ke = k * jnp.exp(tot_rows - gl)
        qd32 = qd * jnp.concatenate(
            [ones_blk, _bcast_rows(e_t[0], HG_BLOCK), ones_blk, _bcast_rows(e_t[2], HG_BLOCK)], axis=0)
        ke32 = ke * jnp.concatenate(
            [_bcast_rows(e_t[1], HG_BLOCK), ones_blk, _bcast_rows(e_t[3], HG_BLOCK), ones_blk], axis=0)
        qd64 = qd32 * jnp.concatenate(
            [ones_blk, ones_blk, _bcast_rows(e01, HG_BLOCK), _bcast_rows(e01, HG_BLOCK)], axis=0)
        ke64 = ke32 * jnp.concatenate(
            [_bcast_rows(e23, HG_BLOCK), _bcast_rows(e23, HG_BLOCK), ones_blk, ones_blk], axis=0)
        qd, ki, ke, qd32, ke32, qd64, ke64 = (
            t.astype(BF16) for t in (qd, ki, ke, qd32, ke32, qd64, ke64))
        prepped.append((v, dec, qd64, ke64))
        for h in range(HG_HEADS):
            kl = slice(h * HG_DK, (h + 1) * HG_DK)
            lhs = jnp.concatenate([qd[:, kl], qd32[:, kl]], axis=1)
            rhs_t = jnp.concatenate([jnp.concatenate([ki[:, kl], zero_blk], axis=1),
                                     jnp.concatenate([ke[:, kl], zero_blk], axis=1),
                                     jnp.concatenate([zero_blk, ke32[:, kl]], axis=1)], axis=0)
            atts[ci, h] = jnp.where(m_all, _dot_nt(lhs, rhs_t), 0.0).astype(BF16)
    for ci in range(SEQ_TILE // CHUNK):
        rows = slice(ci * CHUNK, (ci + 1) * CHUNK)
        v, dec, qd64, ke64 = prepped[ci]
        for h in range(HG_HEADS):
            kl = slice(h * HG_DK, (h + 1) * HG_DK)
            vl = slice(h * HG_DV, (h + 1) * HG_DV)
            st = st_ref[h]
            v3 = jnp.concatenate([v[:, vl]] * 3, axis=0)
            oacc_ref[rows, vl] = _dot(atts[ci, h], v3) + _dot_nt(qd64[:, kl], st.astype(BF16))
            st_ref[h] = st * dec[:, kl] + _dot_tn(v[:, vl], ke64[:, kl])

    gate = _dot(n, w_ref[:, 2 * HG_KDIM + HG_VDIM:2 * HG_KDIM + 2 * HG_VDIM])
    gn = gn_ref[...]
    for h in range(HG_HEADS):
        vl = slice(h * HG_DV, (h + 1) * HG_DV)
        o_ref[:, vl] = (_rms(oacc_ref[:, vl], gn[:, vl]) * _silu(gate[:, vl])).astype(BF16)


def _hgrn(n, w, hg_lb, hg_norm, batch, seq):
    tiles = seq // SEQ_TILE
    seq_spec = pl.BlockSpec((SEQ_TILE, D_MODEL), lambda b, j: (b * tiles + j, 0))
    return pl.pallas_call(
        _hgrn_kernel,
        grid=(batch, tiles),
        in_specs=[seq_spec, _resident(w.shape), _resident(hg_lb.shape), _resident((1, HG_VDIM))],
        out_specs=pl.BlockSpec((SEQ_TILE, HG_VDIM), lambda b, j: (b * tiles + j, 0)),
        out_shape=jax.ShapeDtypeStruct((batch * seq, HG_VDIM), BF16),
        scratch_shapes=[
            pltpu.VMEM((HG_HEADS, HG_DV, HG_DK), F32),
            pltpu.VMEM((SEQ_TILE, HG_KDIM), F32),
            pltpu.VMEM((SEQ_TILE, HG_KDIM), F32),
            pltpu.VMEM((SEQ_TILE, HG_KDIM), F32),
            pltpu.VMEM((SEQ_TILE, HG_VDIM), BF16),
            pltpu.VMEM((SEQ_TILE, HG_VDIM), F32),
        ],
        compiler_params=pltpu.CompilerParams(dimension_semantics=("arbitrary", "arbitrary"),
                                             vmem_limit_bytes=VMEM_LIMIT),
        name="hgrn2",
    )(n, w, hg_lb, hg_norm)


_PAIRS = M_HEADS // 2
_PAIR_W = 2 * M_HEADDIM


def _ssd_kernel(n_ref, wz_ref, wxbc_ref, wdt_ref, wdtt_ref, cw_ref, cb_ref, dtb_ref, aneg_ref,
                dtbt_ref, anegt_ref, dsk_ref, gn_ref, y_ref,
                st_ref, hist_ref, edge_ref, xbc_ref, yacc_ref):
    @pl.when(pl.program_id(1) == 0)
    def _():
        st_ref[...] = jnp.zeros_like(st_ref)
        hist_ref[...] = jnp.zeros_like(hist_ref)

    n = n_ref[...]
    x = _dot(n, wxbc_ref[...])
    cw = cw_ref[...]
    cb = cb_ref[...]
    x1 = pltpu.roll(x, 1, 0)
    near = x * cw[3:4] + x1 * cw[2:3]
    far = x * cw[1:2] + x1 * cw[0:1]
    xbc_ref[...] = _silu(near + pltpu.roll(far, 2, 0) + cb)
    edge_ref[0:8, :] = hist_ref[...]
    edge_ref[8:16, :] = x[0:8]
    top = _bcast_rows(cb, 8)
    for kk in range(M_CONV):
        top = top + edge_ref[8 - (M_CONV - 1) + kk:16 - (M_CONV - 1) + kk, :] * cw[kk:kk + 1]
    xbc_ref[0:8, :] = _silu(top)
    hist_ref[...] = x[SEQ_TILE - 8:SEQ_TILE]

    r = lax.broadcasted_iota(jnp.int32, (SEQ_TILE, SEQ_TILE), 0)
    c = lax.broadcasted_iota(jnp.int32, (SEQ_TILE, SEQ_TILE), 1)
    same_chunk = c // CHUNK == r // CHUNK
    tril_chunk = ((c <= r) & same_chunk).astype(BF16)
    triu_chunk = ((r <= c) & same_chunk).astype(BF16)
    e_r = lax.broadcasted_iota(jnp.int32, (_PAIR_W, M_DINNER), 0)
    e_c = lax.broadcasted_iota(jnp.int32, (_PAIR_W, M_DINNER), 1)
    expand = (e_c // M_HEADDIM == e_r).astype(BF16)

    dt_c = _softplus(_dot(n, wdt_ref[...]) + dtb_ref[...])
    acs_c = _dot_sel(tril_chunk, dt_c * aneg_ref[...])
    dt_e = _dot_sel_r(dt_c, expand, 2)
    acs_e = _dot_sel_r(acs_c, expand, 3)
    a_t = _softplus(_dot_nt(wdtt_ref[...], n) + dtbt_ref[...]) * anegt_ref[...]
    acs_t = _dot_sel_r(a_t, triu_chunk, 3)

    lane_left = lax.broadcasted_iota(jnp.int32, (_PAIRS, _PAIR_W), 1) < CHUNK
    acs_pair = []
    for vc in range(SEQ_TILE // _PAIR_W):
        ev = acs_t[0:_PAIRS, vc * _PAIR_W:(vc + 1) * _PAIR_W]
        od = acs_t[_PAIRS:2 * _PAIRS, vc * _PAIR_W:(vc + 1) * _PAIR_W]
        acs_pair.append(jnp.where(lane_left, ev, pltpu.roll(od, CHUNK, 1)))
        acs_pair.append(jnp.where(lane_left, pltpu.roll(ev, CHUNK, 1), od))

    xdt = xbc_ref[:, 0:M_DINNER] * dt_e
    w_start = jnp.exp(acs_e)

    t_i = lax.broadcasted_iota(jnp.int32, (CHUNK, _PAIR_W), 0)
    l_i = lax.broadcasted_iota(jnp.int32, (CHUNK, _PAIR_W), 1)
    causal2 = (l_i % CHUNK) <= t_i
    left = l_i < M_HEADDIM

    bc = {}
    for ci in range(SEQ_TILE // CHUNK):
        rows = slice(ci * CHUNK, (ci + 1) * CHUNK)
        acs = acs_e[rows]
        xdt_c = xdt[rows].astype(BF16)
        for g in range(M_GROUPS):
            b_g = xbc_ref[rows, M_DINNER + g * M_STATE:M_DINNER + (g + 1) * M_STATE].astype(BF16)
            c_g = xbc_ref[rows, M_DINNER + (M_GROUPS + g) * M_STATE:
                          M_DINNER + (M_GROUPS + g + 1) * M_STATE].astype(BF16)
            bc[ci, g] = (b_g, c_g)
            cb2 = _dot_nt(c_g, jnp.concatenate([b_g, b_g], axis=0))
            for pp in range(M_GROUP_W // _PAIR_W):
                p = g * (M_GROUP_W // _PAIR_W) + pp
                pl_ = slice(p * _PAIR_W, (p + 1) * _PAIR_W)
                seg = acs[:, pl_] - acs_pair[ci][p:p + 1, :]
                decay = jnp.exp(jnp.where(causal2, seg, -jnp.inf))
                m = (cb2 * decay).astype(BF16)
                xp = xdt_c[:, pl_]
                zero = jnp.zeros_like(xp)
                rhs = jnp.concatenate([jnp.where(left, xp, zero), jnp.where(left, zero, xp)], axis=0)
                yacc_ref[rows, pl_] = _dot(m, rhs)
    for ci in range(SEQ_TILE // CHUNK):
        rows = slice(ci * CHUNK, (ci + 1) * CHUNK)
        acs_last = acs_e[(ci + 1) * CHUNK - 1:(ci + 1) * CHUNK]
        xw = (xdt[rows] * jnp.exp(acs_last - acs_e[rows])).astype(BF16)
        dl = jnp.exp(acs_last)
        for g in range(M_GROUPS):
            b_g, c_g = bc[ci, g]
            gl = slice(g * M_GROUP_W, (g + 1) * M_GROUP_W)
            st = st_ref[g]
            yacc_ref[rows, gl] += _dot(c_g, st.astype(BF16)) * w_start[rows, gl]
            st_ref[g] = st * dl[:, gl] + _dot_tn(b_g, xw[:, gl])

    z = _dot(n, wz_ref[...])
    y = (yacc_ref[...] + dsk_ref[...] * xbc_ref[:, 0:M_DINNER]) * _silu(z)
    gn = gn_ref[...]
    for g in range(M_GROUPS):
        gl = slice(g * M_GROUP_W, (g + 1) * M_GROUP_W)
        y_ref[:, gl] = _rms(y[:, gl], gn[:, gl]).astype(BF16)


def _ssd(n, wz, wxbc, wdt, wdtt, cw, cb, dtb, aneg, dtbt, anegt, dsk, gn, batch, seq):
    tiles = seq // SEQ_TILE
    seq_spec = pl.BlockSpec((SEQ_TILE, D_MODEL), lambda b, j: (b * tiles + j, 0))
    params = (wz, wxbc, wdt, wdtt, cw, cb, dtb, aneg, dtbt, anegt, dsk, gn)
    return pl.pallas_call(
        _ssd_kernel,
        grid=(batch, tiles),
        in_specs=[seq_spec] + [_resident(a.shape) for a in params],
        out_specs=pl.BlockSpec((SEQ_TILE, M_DINNER), lambda b, j: (b * tiles + j, 0)),
        out_shape=jax.ShapeDtypeStruct((batch * seq, M_DINNER), BF16),
        scratch_shapes=[
            pltpu.VMEM((M_GROUPS, M_STATE, M_GROUP_W), F32),
            pltpu.VMEM((8, M_CONV_DIM), F32),
            pltpu.VMEM((16, M_CONV_DIM), F32),
            pltpu.VMEM((SEQ_TILE, M_CONV_DIM), F32),
            pltpu.VMEM((SEQ_TILE, M_DINNER), F32),
        ],
        compiler_params=pltpu.CompilerParams(dimension_semantics=("arbitrary", "arbitrary"),
                                             vmem_limit_bytes=VMEM_LIMIT),
        name="ssd",
    )(n, *params)


def _merge_kernel(h_ref, n_ref, ohg_ref, y_ref, wbr_ref, whg_ref, wssm_ref, wout_ref, o_ref):
    br = _sigmoid(_dot(n_ref[...], wbr_ref[...]))
    mixed = (br[:, 0:D_MODEL] * _dot(ohg_ref[...], whg_ref[...])
             + br[:, D_MODEL:2 * D_MODEL] * _dot(y_ref[...], wssm_ref[...]))
    o_ref[...] = h_ref[...] + _dot(mixed.astype(BF16), wout_ref[...])


def _merge(h, n, ohg, y, wbr, whg, wssm, wout):
    t = h.shape[0]
    return pl.pallas_call(
        _merge_kernel,
        grid=(t // ROW_TILE,),
        in_specs=[_row_spec(D_MODEL), _row_spec(D_MODEL), _row_spec(HG_VDIM), _row_spec(M_DINNER),
                  _resident(wbr.shape), _resident(whg.shape), _resident(wssm.shape),
                  _resident(wout.shape)],
        out_specs=_row_spec(D_MODEL),
        out_shape=jax.ShapeDtypeStruct((t, D_MODEL), F32),
        compiler_params=pltpu.CompilerParams(dimension_semantics=("arbitrary",),
                                             vmem_limit_bytes=VMEM_LIMIT),
        name="merge",
    )(h, n, ohg, y, wbr, whg, wssm, wout)


def kernel(x, p, ffn1_norm, ffn1_w13, ffn1_w2, mix_norm, w_in, conv_w, conv_b, dt_bias, a_log,
           d_skip, ssm_norm, hg_lb, hg_norm, w_hg_out, w_ssm_out, w_out, ffn2_norm, ffn2_w13,
           ffn2_w2, ple_norm, w_ple_gate, w_ple_proj, final_norm):
    batch, seq, _ = x.shape
    assert ffn1_w13.shape[0] == 1 and hg_lb.shape[0] == 2, "single-layer block"
    assert seq % SEQ_TILE == 0 and (batch * seq) % ROW_TILE == 0
    t = batch * seq
    row = lambda v: v.reshape(1, -1).astype(F32)
    col = lambda v: v.reshape(-1, 1).astype(F32)
    expand = lambda v: jnp.repeat(v, M_HEADDIM, axis=-1)

    sizes = (HG_KDIM, HG_KDIM, HG_VDIM, HG_VDIM, M_DINNER, M_CONV_DIM, M_HEADS, 2 * D_MODEL)
    offs = [0]
    for s in sizes:
        offs.append(offs[-1] + s)
    w = w_in[0]
    w_hg = w[:, offs[0]:offs[4]].astype(BF16)
    w_z = w[:, offs[4]:offs[5]].astype(BF16)
    w_xbc = w[:, offs[5]:offs[6]].astype(BF16)
    w_dt = w[:, offs[6]:offs[7]].astype(BF16)
    w_br = w[:, offs[7]:offs[8]].astype(BF16)
    a_neg = -jnp.exp(a_log[0].astype(F32))
    dtb = dt_bias[0].astype(F32)

    x2 = x.reshape(t, D_MODEL)
    h1, n = _ffn1(x2, row(ffn1_norm[0]), ffn1_w13[0].astype(BF16), ffn1_w2[0].astype(BF16),
                  row(mix_norm[0]))
    o_hg = _hgrn(n, w_hg, hg_lb.astype(F32), row(hg_norm[0]), batch, seq)
    pad_heads = lambda v: jnp.pad(v, [(0, 0)] * (v.ndim - 1) + [(0, _PAIR_W - M_HEADS)])
    even_odd = lambda v: jnp.concatenate([v[..., 0::2], v[..., 1::2]], axis=-1)
    y = _ssd(n, w_z, w_xbc, pad_heads(w_dt), even_odd(w_dt).T,
             conv_w[0].astype(F32), row(conv_b[0]), row(pad_heads(dtb)), row(pad_heads(a_neg)),
             col(even_odd(dtb)), col(even_odd(a_neg)),
             row(expand(d_skip[0].astype(F32))), row(ssm_norm[0]), batch, seq)
    h2 = _merge(h1, n, o_hg, y, w_br, w_hg_out[0].astype(BF16), w_ssm_out[0].astype(BF16),
                w_out[0].astype(BF16))
    out = _ffn2(h2, row(ffn2_norm[0]), ffn2_w13[0].astype(BF16), ffn2_w2[0].astype(BF16),
                p[0].reshape(t, PLE_DIM), row(ple_norm[0]), w_ple_gate[0].astype(BF16),
                w_ple_proj[0].astype(BF16), row(final_norm))
    return out.reshape(batch, seq, D_MODEL)
```

```python
import jax
import jax.numpy as jnp
from jax import lax
from jax.experimental import pallas as pl
from jax.experimental.pallas import tpu as pltpu

F32 = jnp.float32
BF16 = jnp.bfloat16

D_MODEL = 1024
D_FF = 2816
PLE_DIM = 256
EPS = 1e-6
HG_HEADS = 8
HG_DK = 128
HG_DV = 128
HG_KDIM = HG_HEADS * HG_DK
HG_VDIM = HG_HEADS * HG_DV
HG_BLOCK = 16
M_DINNER = 2048
M_HEADDIM = 64
M_HEADS = 32
M_STATE = 128
M_GROUPS = 4
M_CONV = 4
M_GROUP_W = M_DINNER // M_GROUPS
M_CONV_DIM = M_DINNER + 2 * M_GROUPS * M_STATE
CHUNK = 64

ROW_TILE = 512
SEQ_TILE = 256
N_CHUNKS = SEQ_TILE // CHUNK
COL_BLOCK = 512
VMEM_LIMIT = 56 * 1024 * 1024


def _dot(a, b):
    return jnp.dot(a, b, preferred_element_type=F32)


def _dot_nt(a, b):
    return lax.dot_general(a, b, (((1,), (1,)), ((), ())), preferred_element_type=F32)


def _dot_tn(a, b):
    return lax.dot_general(a, b, (((0,), (0,)), ((), ())), preferred_element_type=F32)


def _split_bf16(x, parts):
    out, rem = [], x
    for _ in range(parts):
        p = rem.astype(BF16)
        out.append(p)
        rem = rem - p.astype(F32)
    return out


def _dot_sel(sel, x, parts=3):
    acc = None
    for p in _split_bf16(x, parts):
        t = _dot(sel, p)
        acc = t if acc is None else acc + t
    return acc


def _dot_sel_r(x, sel, parts=3):
    acc = None
    for p in _split_bf16(x, parts):
        t = _dot(p, sel)
        acc = t if acc is None else acc + t
    return acc


def _rms(x, g):
    ms = jnp.mean(x * x, axis=-1, keepdims=True)
    return x * lax.rsqrt(ms + EPS) * g


def _sigmoid(x):
    return 0.5 * jnp.tanh(0.5 * x) + 0.5


def _silu(x):
    h = 0.5 * x
    return h + h * jnp.tanh(h)


def _softplus(x):
    return jnp.maximum(x, 0.0) + jnp.log(1.0 + jnp.exp(-jnp.abs(x)))


def _bcast_rows(row, rows):
    return jnp.broadcast_to(row, (rows, row.shape[-1]))


def _resident(shape):
    nd = len(shape)
    return pl.BlockSpec(shape, lambda *_: (0,) * nd, pipeline_mode=pl.Buffered(1))


def _swiglu_residual(x, g_ref, w13_ref, w2_ref):
    n = _rms(x, g_ref[...]).astype(BF16)
    gate = _dot(n, w13_ref[:, 0:D_FF])
    up = _dot(n, w13_ref[:, D_FF:2 * D_FF])
    act = (_silu(gate) * up).astype(BF16)
    return x + 0.5 * _dot(act, w2_ref[...])


def _ffn1_kernel(x_ref, g_ref, w13_ref, w2_ref, mixg_ref, h_ref, n_ref):
    h = _swiglu_residual(x_ref[...], g_ref, w13_ref, w2_ref)
    h_ref[...] = h
    n_ref[...] = _rms(h, mixg_ref[...])


def _ffn2_kernel(x_ref, g_ref, w13_ref, w2_ref, p_ref, pleg_ref, wpg_ref, wpp_ref, fing_ref, o_ref):
    h = _swiglu_residual(x_ref[...], g_ref, w13_ref, w2_ref)
    gate = _sigmoid(_dot(_rms(h, pleg_ref[...]).astype(BF16), wpg_ref[...]))
    emb = _dot(p_ref[...].astype(BF16), wpp_ref[...])
    h = h + gate * emb
    o_ref[...] = _rms(h, fing_ref[...])


def _row_spec(width):
    return pl.BlockSpec((ROW_TILE, width), lambda i: (i, 0))


def _ffn1(x, g, w13, w2, mixg):
    t = x.shape[0]
    return pl.pallas_call(
        _ffn1_kernel,
        grid=(t // ROW_TILE,),
        in_specs=[_row_spec(D_MODEL), _resident((1, D_MODEL)), _resident(w13.shape),
                  _resident(w2.shape), _resident((1, D_MODEL))],
        out_specs=[_row_spec(D_MODEL), _row_spec(D_MODEL)],
        out_shape=[jax.ShapeDtypeStruct((t, D_MODEL), F32),
                   jax.ShapeDtypeStruct((t, D_MODEL), F32)],
        compiler_params=pltpu.CompilerParams(dimension_semantics=("arbitrary",),
                                             vmem_limit_bytes=VMEM_LIMIT),
        name="ffn1",
    )(x, g, w13, w2, mixg)


def _ffn2(x, g, w13, w2, p, pleg, wpg, wpp, fing):
    t = x.shape[0]
    return pl.pallas_call(
        _ffn2_kernel,
        grid=(t // ROW_TILE,),
        in_specs=[_row_spec(D_MODEL), _resident((1, D_MODEL)), _resident(w13.shape),
                  _resident(w2.shape), _row_spec(PLE_DIM), _resident((1, D_MODEL)),
                  _resident(wpg.shape), _resident(wpp.shape), _resident((1, D_MODEL))],
        out_specs=_row_spec(D_MODEL),
        out_shape=jax.ShapeDtypeStruct((t, D_MODEL), F32),
        compiler_params=pltpu.CompilerParams(dimension_semantics=("arbitrary",),
                                             vmem_limit_bytes=VMEM_LIMIT),
        name="ffn2",
    )(x, g, w13, w2, p, pleg, wpg, wpp, fing)


def _hgrn_kernel(n_ref, w_ref, lb_ref, gn_ref, o_ref,
                 st_ref, q_ref, k_ref, gl_ref, v_ref, gate_ref, oacc_ref):
    @pl.when(pl.program_id(1) == 0)
    def _():
        st_ref[...] = jnp.zeros_like(st_ref)

    n = n_ref[...].astype(BF16)
    a = lb_ref[...]
    e = jnp.exp(a - jnp.max(a, axis=0, keepdims=True))
    lb = e[0:1] / jnp.sum(e, axis=0, keepdims=True)

    r = lax.broadcasted_iota(jnp.int32, (SEQ_TILE, SEQ_TILE), 0)
    c = lax.broadcasted_iota(jnp.int32, (SEQ_TILE, SEQ_TILE), 1)
    tril_blk = ((c <= r) & (c // HG_BLOCK == r // HG_BLOCK)).astype(BF16)
    for cb in range(HG_KDIM // COL_BLOCK):
        cols = slice(cb * COL_BLOCK, (cb + 1) * COL_BLOCK)
        f_raw = _dot(n, w_ref[:, HG_KDIM + cb * COL_BLOCK:HG_KDIM + (cb + 1) * COL_BLOCK])
        f = lb[:, cols] + (1.0 - lb[:, cols]) * _sigmoid(f_raw)
        k_ref[:, cols] = 1.0 - f
        gl_ref[:, cols] = _dot_sel(tril_blk, jnp.log(f))
    q_ref[...] = _dot(n, w_ref[:, 0:HG_KDIM]) * (HG_DK ** -0.5)
    v_ref[...] = _dot(n, w_ref[:, 2 * HG_KDIM:2 * HG_KDIM + HG_VDIM]).astype(BF16)

    t_i = lax.broadcasted_iota(jnp.int32, (CHUNK, 3 * CHUNK), 0)
    c_i = lax.broadcasted_iota(jnp.int32, (CHUNK, 3 * CHUNK), 1)
    s_i, grp = c_i % CHUNK, c_i // CHUNK
    bt, bs = t_i // HG_BLOCK, s_i // HG_BLOCK
    m_all = (((grp == 0) & (bt == bs) & (s_i <= t_i))
             | ((grp == 1) & (bt % 2 == 1) & (bs == bt - 1))
             | ((grp == 2) & (bt >= 2) & (bs < 2)))
    zero_blk = jnp.zeros((CHUNK, HG_DK), BF16)
    ones_blk = jnp.ones((HG_BLOCK, HG_KDIM), F32)
    gate_cols = HG_VDIM // N_CHUNKS

    atts = {}
    prepped = []
    for ci in range(N_CHUNKS):
        rows = slice(ci * CHUNK, (ci + 1) * CHUNK)
        q, k, gl, v = q_ref[rows, :], k_ref[rows, :], gl_ref[rows, :], v_ref[rows, :]
        tot = [gl[HG_BLOCK * i + HG_BLOCK - 1:HG_BLOCK * (i + 1), :] for i in range(4)]
        e_t = [jnp.exp(t) for t in tot]
        e01, e23 = e_t[0] * e_t[1], e_t[2] * e_t[3]
        dec = e01 * e23
        tot_rows = jnp.concatenate([_bcast_rows(t, HG_BLOCK) for t in tot], axis=0)
        qd = q * jnp.exp(gl)
        ki = k * jnp.exp(-gl)
        ke = k * jnp.exp(tot_rows - gl)
        qd32 = qd * jnp.concatenate(
            [ones_blk, _bcast_rows(e_t[0], HG_BLOCK), ones_blk, _bcast_rows(e_t[2], HG_BLOCK)], axis=0)
        ke32 = ke * jnp.concatenate(
            [_bcast_rows(e_t[1], HG_BLOCK), ones_blk, _bcast_rows(e_t[3], HG_BLOCK), ones_blk], axis=0)
        qd64 = qd32 * jnp.concatenate(
            [ones_blk, ones_blk, _bcast_rows(e01, HG_BLOCK), _bcast_rows(e01, HG_BLOCK)], axis=0)
        ke64 = ke32 * jnp.concatenate(
            [_bcast_rows(e23, HG_BLOCK), _bcast_rows(e23, HG_BLOCK), ones_blk, ones_blk], axis=0)
        qd, ki, ke, qd32, ke32, qd64, ke64 = (
            t.astype(BF16) for t in (qd, ki, ke, qd32, ke32, qd64, ke64))
        prepped.append((v, dec, qd64, ke64))
        g0 = 2 * HG_KDIM + HG_VDIM + ci * gate_cols
        gate_ref[:, ci * gate_cols:(ci + 1) * gate_cols] = _silu(_dot(n, w_ref[:, g0:g0 + gate_cols]))
        for h in range(HG_HEADS):
            kl = slice(h * HG_DK, (h + 1) * HG_DK)
            lhs = jnp.concatenate([qd[:, kl], qd32[:, kl]], axis=1)
            rhs_t = jnp.concatenate([jnp.concatenate([ki[:, kl], zero_blk], axis=1),
                                     jnp.concatenate([ke[:, kl], zero_blk], axis=1),
                                     jnp.concatenate([zero_blk, ke32[:, kl]], axis=1)], axis=0)
            atts[ci, h] = jnp.where(m_all, _dot_nt(lhs, rhs_t), 0.0).astype(BF16)
    for ci in range(N_CHUNKS):
        rows = slice(ci * CHUNK, (ci + 1) * CHUNK)
        v, dec, qd64, ke64 = prepped[ci]
        for h in range(HG_HEADS):
            kl = slice(h * HG_DK, (h + 1) * HG_DK)
            vl = slice(h * HG_DV, (h + 1) * HG_DV)
            st = st_ref[h]
            v3 = jnp.concatenate([v[:, vl]] * 3, axis=0)
            oacc_ref[rows, vl] = _dot(atts[ci, h], v3) + _dot_nt(qd64[:, kl], st.astype(BF16))
            st_ref[h] = st * dec[:, kl] + _dot_tn(v[:, vl], ke64[:, kl])

    gn = gn_ref[...]
    for h in range(HG_HEADS):
        vl = slice(h * HG_DV, (h + 1) * HG_DV)
        o_ref[:, vl] = (_rms(oacc_ref[:, vl], gn[:, vl]) * gate_ref[:, vl]).astype(BF16)


def _hgrn(n, w, hg_lb, hg_norm, batch, seq):
    tiles = seq // SEQ_TILE
    seq_spec = pl.BlockSpec((SEQ_TILE, D_MODEL), lambda b, j: (b * tiles + j, 0))
    return pl.pallas_call(
        _hgrn_kernel,
        grid=(batch, tiles),
        in_specs=[seq_spec, _resident(w.shape), _resident(hg_lb.shape), _resident((1, HG_VDIM))],
        out_specs=pl.BlockSpec((SEQ_TILE, HG_VDIM), lambda b, j: (b * tiles + j, 0)),
        out_shape=jax.ShapeDtypeStruct((batch * seq, HG_VDIM), BF16),
        scratch_shapes=[
            pltpu.VMEM((HG_HEADS, HG_DV, HG_DK), F32),
            pltpu.VMEM((SEQ_TILE, HG_KDIM), F32),
            pltpu.VMEM((SEQ_TILE, HG_KDIM), F32),
            pltpu.VMEM((SEQ_TILE, HG_KDIM), F32),
            pltpu.VMEM((SEQ_TILE, HG_VDIM), BF16),
            pltpu.VMEM((SEQ_TILE, HG_VDIM), F32),
            pltpu.VMEM((SEQ_TILE, HG_VDIM), F32),
        ],
        compiler_params=pltpu.CompilerParams(dimension_semantics=("arbitrary", "arbitrary"),
                                             vmem_limit_bytes=VMEM_LIMIT),
        name="hgrn2",
    )(n, w, hg_lb, hg_norm)


_PAIRS = M_HEADS // 2
_PAIR_W = 2 * M_HEADDIM


def _ssd_kernel(n_ref, wz_ref, wxbc_ref, wdt_ref, wdtt_ref, cw_ref, cb_ref, dtb_ref, aneg_ref,
                dtbt_ref, anegt_ref, dsk_ref, gn_ref, y_ref,
                st_ref, hist_ref, edge_ref, xbc_ref, zs_ref, yacc_ref):
    @pl.when(pl.program_id(1) == 0)
    def _():
        st_ref[...] = jnp.zeros_like(st_ref)
        hist_ref[...] = jnp.zeros_like(hist_ref)

    n = n_ref[...].astype(BF16)
    for cb_i in range(M_CONV_DIM // COL_BLOCK):
        cols = slice(cb_i * COL_BLOCK, (cb_i + 1) * COL_BLOCK)
        x = _dot(n, wxbc_ref[:, cols])
        cw = cw_ref[:, cols]
        cb = cb_ref[:, cols]
        x1 = pltpu.roll(x, 1, 0)
        near = x * cw[3:4] + x1 * cw[2:3]
        far = x * cw[1:2] + x1 * cw[0:1]
        xbc_ref[:, cols] = _silu(near + pltpu.roll(far, 2, 0) + cb)
        edge_ref[0:8, cols] = hist_ref[:, cols]
        edge_ref[8:16, cols] = x[0:8]
        top = _bcast_rows(cb, 8)
        for kk in range(M_CONV):
            top = top + edge_ref[8 - (M_CONV - 1) + kk:16 - (M_CONV - 1) + kk, cols] * cw[kk:kk + 1]
        xbc_ref[0:8, cols] = _silu(top)
        hist_ref[:, cols] = x[SEQ_TILE - 8:SEQ_TILE]

    r = lax.broadcasted_iota(jnp.int32, (SEQ_TILE, SEQ_TILE), 0)
    c = lax.broadcasted_iota(jnp.int32, (SEQ_TILE, SEQ_TILE), 1)
    same_chunk = c // CHUNK == r // CHUNK
    tril_chunk = ((c <= r) & same_chunk).astype(BF16)
    triu_chunk = ((r <= c) & same_chunk).astype(BF16)
    e_r = lax.broadcasted_iota(jnp.int32, (_PAIR_W, M_DINNER), 0)
    e_c = lax.broadcasted_iota(jnp.int32, (_PAIR_W, M_DINNER), 1)
    expand = (e_c // M_HEADDIM == e_r).astype(BF16)

    dt_c = _softplus(_dot(n, wdt_ref[...]) + dtb_ref[...])
    acs_c = _dot_sel(tril_chunk, dt_c * aneg_ref[...])
    dt_e = _dot_sel_r(dt_c, expand, 2)
    acs_e = _dot_sel_r(acs_c, expand, 3)
    a_t = _softplus(_dot_nt(wdtt_ref[...], n) + dtbt_ref[...]) * anegt_ref[...]
    acs_t = _dot_sel_r(a_t, triu_chunk, 3)

    lane_left = lax.broadcasted_iota(jnp.int32, (_PAIRS, _PAIR_W), 1) < CHUNK
    acs_pair = []
    for vc in range(SEQ_TILE // _PAIR_W):
        ev = acs_t[0:_PAIRS, vc * _PAIR_W:(vc + 1) * _PAIR_W]
        od = acs_t[_PAIRS:2 * _PAIRS, vc * _PAIR_W:(vc + 1) * _PAIR_W]
        acs_pair.append(jnp.where(lane_left, ev, pltpu.roll(od, CHUNK, 1)))
        acs_pair.append(jnp.where(lane_left, pltpu.roll(ev, CHUNK, 1), od))

    xdt = xbc_ref[:, 0:M_DINNER] * dt_e
    w_start = jnp.exp(acs_e)

    t_i = lax.broadcasted_iota(jnp.int32, (CHUNK, _PAIR_W), 0)
    l_i = lax.broadcasted_iota(jnp.int32, (CHUNK, _PAIR_W), 1)
    causal2 = (l_i % CHUNK) <= t_i
    left = l_i < M_HEADDIM

    row_chunk = lax.broadcasted_iota(jnp.int32, (SEQ_TILE, M_STATE), 0) // CHUNK
    acs_last = [acs_e[(ci + 1) * CHUNK - 1:(ci + 1) * CHUNK] for ci in range(N_CHUNKS)]
    acs_last_rows = jnp.concatenate([_bcast_rows(a, CHUNK) for a in acs_last], axis=0)
    xw = (xdt * jnp.exp(acs_last_rows - acs_e)).astype(BF16)
    xdt_b = xdt.astype(BF16)

    def by_chunk(v):
        zero = jnp.zeros_like(v)
        return jnp.concatenate([jnp.where(row_chunk == ci, v, zero) for ci in range(N_CHUNKS)], axis=1)

    for g in range(M_GROUPS):
        gl = slice(g * M_GROUP_W, (g + 1) * M_GROUP_W)
        b_g = xbc_ref[:, M_DINNER + g * M_STATE:M_DINNER + (g + 1) * M_STATE].astype(BF16)
        c_g = xbc_ref[:, M_DINNER + (M_GROUPS + g) * M_STATE:
                      M_DINNER + (M_GROUPS + g + 1) * M_STATE].astype(BF16)
        zs_ref[:, gl] = _silu(_dot(n, wz_ref[:, gl]))
        b_dup = jnp.concatenate([b_g[ci * CHUNK:(ci + 1) * CHUNK] for ci in range(N_CHUNKS) for _ in (0, 1)],
                                axis=0)
        cb_all = _dot_nt(c_g, b_dup)
        for ci in range(N_CHUNKS):
            rows = slice(ci * CHUNK, (ci + 1) * CHUNK)
            cb2 = cb_all[rows, ci * _PAIR_W:(ci + 1) * _PAIR_W]
            for pp in range(M_GROUP_W // _PAIR_W):
                p = g * (M_GROUP_W // _PAIR_W) + pp
                pl_ = slice(p * _PAIR_W, (p + 1) * _PAIR_W)
                seg = acs_e[rows, pl_] - acs_pair[ci][p:p + 1, :]
                decay = jnp.exp(jnp.where(causal2, seg, -jnp.inf))
                m = (cb2 * decay).astype(BF16)
                xp = xdt_b[rows, pl_]
                zero = jnp.zeros_like(xp)
                rhs = jnp.concatenate([jnp.where(left, xp, zero), jnp.where(left, zero, xp)], axis=0)
                yacc_ref[rows, pl_] = _dot(m, rhs)
        incr = _dot_tn(by_chunk(b_g), xw[:, gl])
        st = st_ref[g]
        starts = []
        for ci in range(N_CHUNKS):
            starts.append(st.astype(BF16))
            st = st * jnp.exp(acs_last[ci][:, gl]) + incr[ci * M_STATE:(ci + 1) * M_STATE]
        st_ref[g] = st
        y_inter = _dot(by_chunk(c_g), jnp.concatenate(starts, axis=0))
        yacc_ref[:, gl] += y_inter * w_start[:, gl]

    gn = gn_ref[...]
    for g in range(M_GROUPS):
        gl = slice(g * M_GROUP_W, (g + 1) * M_GROUP_W)
        y = (yacc_ref[:, gl] + dsk_ref[:, gl] * xbc_ref[:, gl]) * zs_ref[:, gl]
        y_ref[:, gl] = _rms(y, gn[:, gl]).astype(BF16)


def _ssd(n, wz, wxbc, wdt, wdtt, cw, cb, dtb, aneg, dtbt, anegt, dsk, gn, batch, seq):
    tiles = seq // SEQ_TILE
    seq_spec = pl.BlockSpec((SEQ_TILE, D_MODEL), lambda b, j: (b * tiles + j, 0))
    params = (wz, wxbc, wdt, wdtt, cw, cb, dtb, aneg, dtbt, anegt, dsk, gn)
    return pl.pallas_call(
        _ssd_kernel,
        grid=(batch, tiles),
        in_specs=[seq_spec] + [_resident(a.shape) for a in params],
        out_specs=pl.BlockSpec((SEQ_TILE, M_DINNER), lambda b, j: (b * tiles + j, 0)),
        out_shape=jax.ShapeDtypeStruct((batch * seq, M_DINNER), BF16),
        scratch_shapes=[
            pltpu.VMEM((M_GROUPS, M_STATE, M_GROUP_W), F32),
            pltpu.VMEM((8, M_CONV_DIM), F32),
            pltpu.VMEM((16, M_CONV_DIM), F32),
            pltpu.VMEM((SEQ_TILE, M_CONV_DIM), F32),
            pltpu.VMEM((SEQ_TILE, M_DINNER), F32),
            pltpu.VMEM((SEQ_TILE, M_DINNER), F32),
        ],
        compiler_params=pltpu.CompilerParams(dimension_semantics=("arbitrary", "arbitrary"),
                                             vmem_limit_bytes=VMEM_LIMIT),
        name="ssd",
    )(n, *params)


def _merge_kernel(h_ref, n_ref, ohg_ref, y_ref, wbr_ref, whg_ref, wssm_ref, wout_ref, o_ref):
    br = _sigmoid(_dot(n_ref[...].astype(BF16), wbr_ref[...]))
    mixed = (br[:, 0:D_MODEL] * _dot(ohg_ref[...], whg_ref[...])
             + br[:, D_MODEL:2 * D_MODEL] * _dot(y_ref[...], wssm_ref[...]))
    o_ref[...] = h_ref[...] + _dot(mixed.astype(BF16), wout_ref[...])


def _merge(h, n, ohg, y, wbr, whg, wssm, wout):
    t = h.shape[0]
    return pl.pallas_call(
        _merge_kernel,
        grid=(t // ROW_TILE,),
        in_specs=[_row_spec(D_MODEL), _row_spec(D_MODEL), _row_spec(HG_VDIM), _row_spec(M_DINNER),
                  _resident(wbr.shape), _resident(whg.shape), _resident(wssm.shape),
                  _resident(wout.shape)],
        out_specs=_row_spec(D_MODEL),
        out_shape=jax.ShapeDtypeStruct((t, D_MODEL), F32),
        compiler_params=pltpu.CompilerParams(dimension_semantics=("arbitrary",),
                                             vmem_limit_bytes=VMEM_LIMIT),
        name="merge",
    )(h, n, ohg, y, wbr, whg, wssm, wout)


def kernel(x, p, ffn1_norm, ffn1_w13, ffn1_w2, mix_norm, w_in, conv_w, conv_b, dt_bias, a_log,
           d_skip, ssm_norm, hg_lb, hg_norm, w_hg_out, w_ssm_out, w_out, ffn2_norm, ffn2_w13,
           ffn2_w2, ple_norm, w_ple_gate, w_ple_proj, final_norm):
    batch, seq, _ = x.shape
    assert ffn1_w13.shape[0] == 1 and hg_lb.shape[0] == 2, "single-layer block"
    assert seq % SEQ_TILE == 0 and (batch * seq) % ROW_TILE == 0
    t = batch * seq
    row = lambda v: v.reshape(1, -1).astype(F32)
    col = lambda v: v.reshape(-1, 1).astype(F32)
    expand = lambda v: jnp.repeat(v, M_HEADDIM, axis=-1)
    pad_heads = lambda v: jnp.pad(v, [(0, 0)] * (v.ndim - 1) + [(0, _PAIR_W - M_HEADS)])
    even_odd = lambda v: jnp.concatenate([v[..., 0::2], v[..., 1::2]], axis=-1)

    sizes = (HG_KDIM, HG_KDIM, HG_VDIM, HG_VDIM, M_DINNER, M_CONV_DIM, M_HEADS, 2 * D_MODEL)
    offs = [0]
    for s in sizes:
        offs.append(offs[-1] + s)
    w = w_in[0]
    w_hg = w[:, offs[0]:offs[4]].astype(BF16)
    w_z = w[:, offs[4]:offs[5]].astype(BF16)
    w_xbc = w[:, offs[5]:offs[6]].astype(BF16)
    w_dt = w[:, offs[6]:offs[7]].astype(BF16)
    w_br = w[:, offs[7]:offs[8]].astype(BF16)
    a_neg = -jnp.exp(a_log[0].astype(F32))
    dtb = dt_bias[0].astype(F32)

    x2 = x.reshape(t, D_MODEL)
    h1, n = _ffn1(x2, row(ffn1_norm[0]), ffn1_w13[0].astype(BF16), ffn1_w2[0].astype(BF16),
                  row(mix_norm[0]))
    o_hg = _hgrn(n, w_hg, hg_lb.astype(F32), row(hg_norm[0]), batch, seq)
    y = _ssd(n, w_z, w_xbc, pad_heads(w_dt), even_odd(w_dt).T,
             conv_w[0].astype(F32), row(conv_b[0]), row(pad_heads(dtb)), row(pad_heads(a_neg)),
             col(even_odd(dtb)), col(even_odd(a_neg)),
             row(expand(d_skip[0].astype(F32))), row(ssm_norm[0]), batch, seq)
    h2 = _merge(h1, n, o_hg, y, w_br, w_hg_out[0].astype(BF16), w_ssm_out[0].astype(BF16),
                w_out[0].astype(BF16))
    out = _ffn2(h2, row(ffn2_norm[0]), ffn2_w13[0].astype(BF16), ffn2_w2[0].astype(BF16),
                p[0].reshape(t, PLE_DIM), row(ple_norm[0]), w_ple_gate[0].astype(BF16),
                w_ple_proj[0].astype(BF16), row(final_norm))
    return out.reshape(batch, seq, D_MODEL)
```

```python
import jax
import jax.numpy as jnp
from jax import lax
from jax.experimental import pallas as pl
from jax.experimental.pallas import tpu as pltpu

F32 = jnp.float32
BF16 = jnp.bfloat16

D_MODEL = 1024
D_FF = 2816
PLE_DIM = 256
EPS = 1e-6
HG_HEADS = 8
HG_DK = 128
HG_DV = 128
HG_KDIM = HG_HEADS * HG_DK
HG_VDIM = HG_HEADS * HG_DV
HG_BLOCK = 16
M_DINNER = 2048
M_HEADDIM = 64
M_HEADS = 32
M_STATE = 128
M_GROUPS = 4
M_CONV = 4
M_GROUP_W = M_DINNER // M_GROUPS
M_CONV_DIM = M_DINNER + 2 * M_GROUPS * M_STATE
CHUNK = 64

ROW_TILE = 512
SEQ_TILE = 256
N_CHUNKS = SEQ_TILE // CHUNK
COL_BLOCK = 512
VMEM_LIMIT = 56 * 1024 * 1024


def _dot(a, b):
    return jnp.dot(a, b, preferred_element_type=F32)


def _dot_nt(a, b):
    return lax.dot_general(a, b, (((1,), (1,)), ((), ())), preferred_element_type=F32)


def _dot_tn(a, b):
    return lax.dot_general(a, b, (((0,), (0,)), ((), ())), preferred_element_type=F32)


def _split_bf16(x, parts):
    out, rem = [], x
    for _ in range(parts):
        p = rem.astype(BF16)
        out.append(p)
        rem = rem - p.astype(F32)
    return out


def _dot_sel(sel, x, parts=3):
    return _dot(jnp.concatenate([sel] * parts, axis=1), jnp.concatenate(_split_bf16(x, parts), axis=0))


def _dot_sel_r(x, sel, parts=3):
    return _dot(jnp.concatenate(_split_bf16(x, parts), axis=1), jnp.concatenate([sel] * parts, axis=0))


def _rms(x, g):
    ms = jnp.mean(x * x, axis=-1, keepdims=True)
    return x * lax.rsqrt(ms + EPS) * g


def _sigmoid(x):
    return 0.5 * jnp.tanh(0.5 * x) + 0.5


def _silu(x):
    h = 0.5 * x
    return h + h * jnp.tanh(h)


def _softplus(x):
    return jnp.maximum(x, 0.0) + jnp.log(1.0 + jnp.exp(-jnp.abs(x)))


def _bcast_rows(row, rows):
    return jnp.broadcast_to(row, (rows, row.shape[-1]))


def _resident(shape):
    nd = len(shape)
    return pl.BlockSpec(shape, lambda *_: (0,) * nd, pipeline_mode=pl.Buffered(1))


def _swiglu_residual(x, g_ref, w13_ref, w2_ref):
    n = _rms(x, g_ref[...]).astype(BF16)
    gate = _dot(n, w13_ref[:, 0:D_FF])
    up = _dot(n, w13_ref[:, D_FF:2 * D_FF])
    act = (_silu(gate) * up).astype(BF16)
    return x + 0.5 * _dot(act, w2_ref[...])


def _ffn1_kernel(x_ref, g_ref, w13_ref, w2_ref, mixg_ref, h_ref, n_ref):
    h = _swiglu_residual(x_ref[...], g_ref, w13_ref, w2_ref)
    h_ref[...] = h
    n_ref[...] = _rms(h, mixg_ref[...])


def _ffn2_kernel(x_ref, g_ref, w13_ref, w2_ref, p_ref, pleg_ref, wpg_ref, wpp_ref, fing_ref, o_ref):
    h = _swiglu_residual(x_ref[...], g_ref, w13_ref, w2_ref)
    gate = _sigmoid(_dot(_rms(h, pleg_ref[...]).astype(BF16), wpg_ref[...]))
    emb = _dot(p_ref[...].astype(BF16), wpp_ref[...])
    h = h + gate * emb
    o_ref[...] = _rms(h, fing_ref[...])


def _row_spec(width):
    return pl.BlockSpec((ROW_TILE, width), lambda i: (i, 0))


def _ffn1(x, g, w13, w2, mixg):
    t = x.shape[0]
    return pl.pallas_call(
        _ffn1_kernel,
        grid=(t // ROW_TILE,),
        in_specs=[_row_spec(D_MODEL), _resident((1, D_MODEL)), _resident(w13.shape),
                  _resident(w2.shape), _resident((1, D_MODEL))],
        out_specs=[_row_spec(D_MODEL), _row_spec(D_MODEL)],
        out_shape=[jax.ShapeDtypeStruct((t, D_MODEL), F32),
                   jax.ShapeDtypeStruct((t, D_MODEL), F32)],
        compiler_params=pltpu.CompilerParams(dimension_semantics=("arbitrary",),
                                             vmem_limit_bytes=VMEM_LIMIT),
        name="ffn1",
    )(x, g, w13, w2, mixg)


def _ffn2(x, g, w13, w2, p, pleg, wpg, wpp, fing):
    t = x.shape[0]
    return pl.pallas_call(
        _ffn2_kernel,
        grid=(t // ROW_TILE,),
        in_specs=[_row_spec(D_MODEL), _resident((1, D_MODEL)), _resident(w13.shape),
                  _resident(w2.shape), _row_spec(PLE_DIM), _resident((1, D_MODEL)),
                  _resident(wpg.shape), _resident(wpp.shape), _resident((1, D_MODEL))],
        out_specs=_row_spec(D_MODEL),
        out_shape=jax.ShapeDtypeStruct((t, D_MODEL), F32),
        compiler_params=pltpu.CompilerParams(dimension_semantics=("arbitrary",),
                                             vmem_limit_bytes=VMEM_LIMIT),
        name="ffn2",
    )(x, g, w13, w2, p, pleg, wpg, wpp, fing)


def _hgrn_kernel(n_ref, w_ref, lb_ref, gn_ref, o_ref,
                 st_ref, q_ref, k_ref, gl_ref, v_ref, gate_ref, oacc_ref):
    @pl.when(pl.program_id(1) == 0)
    def _():
        st_ref[...] = jnp.zeros_like(st_ref)

    n = n_ref[...].astype(BF16)
    a = lb_ref[...]
    e = jnp.exp(a - jnp.max(a, axis=0, keepdims=True))
    lb = e[0:1] / jnp.sum(e, axis=0, keepdims=True)

    r = lax.broadcasted_iota(jnp.int32, (SEQ_TILE, SEQ_TILE), 0)
    c = lax.broadcasted_iota(jnp.int32, (SEQ_TILE, SEQ_TILE), 1)
    tril_blk = ((c <= r) & (c // HG_BLOCK == r // HG_BLOCK)).astype(BF16)
    for cb in range(HG_KDIM // COL_BLOCK):
        cols = slice(cb * COL_BLOCK, (cb + 1) * COL_BLOCK)
        f_raw = _dot(n, w_ref[:, HG_KDIM + cb * COL_BLOCK:HG_KDIM + (cb + 1) * COL_BLOCK])
        f = lb[:, cols] + (1.0 - lb[:, cols]) * _sigmoid(f_raw)
        k_ref[:, cols] = 1.0 - f
        gl_ref[:, cols] = _dot_sel(tril_blk, jnp.log(f))
    q_ref[...] = _dot(n, w_ref[:, 0:HG_KDIM]) * (HG_DK ** -0.5)
    v_ref[...] = _dot(n, w_ref[:, 2 * HG_KDIM:2 * HG_KDIM + HG_VDIM]).astype(BF16)

    t_i = lax.broadcasted_iota(jnp.int32, (CHUNK, 3 * CHUNK), 0)
    c_i = lax.broadcasted_iota(jnp.int32, (CHUNK, 3 * CHUNK), 1)
    s_i, grp = c_i % CHUNK, c_i // CHUNK
    bt, bs = t_i // HG_BLOCK, s_i // HG_BLOCK
    m_all = (((grp == 0) & (bt == bs) & (s_i <= t_i))
             | ((grp == 1) & (bt % 2 == 1) & (bs == bt - 1))
             | ((grp == 2) & (bt >= 2) & (bs < 2)))
    zero_blk = jnp.zeros((CHUNK, HG_DK), BF16)
    ones_blk = jnp.ones((HG_BLOCK, HG_KDIM), F32)
    gate_cols = HG_VDIM // N_CHUNKS

    atts = {}
    prepped = []
    for ci in range(N_CHUNKS):
        rows = slice(ci * CHUNK, (ci + 1) * CHUNK)
        q, k, gl, v = q_ref[rows, :], k_ref[rows, :], gl_ref[rows, :], v_ref[rows, :]
        tot = [gl[HG_BLOCK * i + HG_BLOCK - 1:HG_BLOCK * (i + 1), :] for i in range(4)]
        e_t = [jnp.exp(t) for t in tot]
        e01, e23 = e_t[0] * e_t[1], e_t[2] * e_t[3]
        dec = e01 * e23
        tot_rows = jnp.concatenate([_bcast_rows(t, HG_BLOCK) for t in tot], axis=0)
        qd = q * jnp.exp(gl)
        ki = k * jnp.exp(-gl)
        ke = k * jnp.exp(tot_rows - gl)
        qd32 = qd * jnp.concatenate(
            [ones_blk, _bcast_rows(e_t[0], HG_BLOCK), ones_blk, _bcast_rows(e_t[2], HG_BLOCK)], axis=0)
        ke32 = ke * jnp.concatenate(
            [_bcast_rows(e_t[1], HG_BLOCK), ones_blk, _bcast_rows(e_t[3], HG_BLOCK), ones_blk], axis=0)
        qd64 = qd32 * jnp.concatenate(
            [ones_blk, ones_blk, _bcast_rows(e01, HG_BLOCK), _bcast_rows(e01, HG_BLOCK)], axis=0)
        ke64 = ke32 * jnp.concatenate(
            [_bcast_rows(e23, HG_BLOCK), _bcast_rows(e23, HG_BLOCK), ones_blk, ones_blk], axis=0)
        qd, ki, ke, qd32, ke32, qd64, ke64 = (
            t.astype(BF16) for t in (qd, ki, ke, qd32, ke32, qd64, ke64))
        prepped.append((v, dec, qd64, ke64))
        g0 = 2 * HG_KDIM + HG_VDIM + ci * gate_cols
        gate_ref[:, ci * gate_cols:(ci + 1) * gate_cols] = _silu(_dot(n, w_ref[:, g0:g0 + gate_cols]))
        for h in range(HG_HEADS):
            kl = slice(h * HG_DK, (h + 1) * HG_DK)
            lhs = jnp.concatenate([qd[:, kl], qd32[:, kl]], axis=1)
            rhs_t = jnp.concatenate([jnp.concatenate([ki[:, kl], zero_blk], axis=1),
                                     jnp.concatenate([ke[:, kl], zero_blk], axis=1),
                                     jnp.concatenate([zero_blk, ke32[:, kl]], axis=1)], axis=0)
            atts[ci, h] = jnp.where(m_all, _dot_nt(lhs, rhs_t), 0.0).astype(BF16)
    for ci in range(N_CHUNKS):
        rows = slice(ci * CHUNK, (ci + 1) * CHUNK)
        v, dec, qd64, ke64 = prepped[ci]
        for h in range(HG_HEADS):
            kl = slice(h * HG_DK, (h + 1) * HG_DK)
            vl = slice(h * HG_DV, (h + 1) * HG_DV)
            st = st_ref[h]
            v3 = jnp.concatenate([v[:, vl]] * 3, axis=0)
            oacc_ref[rows, vl] = _dot(atts[ci, h], v3) + _dot_nt(qd64[:, kl], st.astype(BF16))
            st_ref[h] = st * dec[:, kl] + _dot_tn(v[:, vl], ke64[:, kl])

    gn = gn_ref[...]
    for h in range(HG_HEADS):
        vl = slice(h * HG_DV, (h + 1) * HG_DV)
        o_ref[:, vl] = (_rms(oacc_ref[:, vl], gn[:, vl]) * gate_ref[:, vl]).astype(BF16)


def _hgrn(n, w, hg_lb, hg_norm, batch, seq):
    tiles = seq // SEQ_TILE
    seq_spec = pl.BlockSpec((SEQ_TILE, D_MODEL), lambda b, j: (b * tiles + j, 0))
    return pl.pallas_call(
        _hgrn_kernel,
        grid=(batch, tiles),
        in_specs=[seq_spec, _resident(w.shape), _resident(hg_lb.shape), _resident((1, HG_VDIM))],
        out_specs=pl.BlockSpec((SEQ_TILE, HG_VDIM), lambda b, j: (b * tiles + j, 0)),
        out_shape=jax.ShapeDtypeStruct((batch * seq, HG_VDIM), BF16),
        scratch_shapes=[
            pltpu.VMEM((HG_HEADS, HG_DV, HG_DK), F32),
            pltpu.VMEM((SEQ_TILE, HG_KDIM), F32),
            pltpu.VMEM((SEQ_TILE, HG_KDIM), F32),
            pltpu.VMEM((SEQ_TILE, HG_KDIM), F32),
            pltpu.VMEM((SEQ_TILE, HG_VDIM), BF16),
            pltpu.VMEM((SEQ_TILE, HG_VDIM), F32),
            pltpu.VMEM((SEQ_TILE, HG_VDIM), F32),
        ],
        compiler_params=pltpu.CompilerParams(dimension_semantics=("arbitrary", "arbitrary"),
                                             vmem_limit_bytes=VMEM_LIMIT),
        name="hgrn2",
    )(n, w, hg_lb, hg_norm)


_PAIRS = M_HEADS // 2
_PAIR_W = 2 * M_HEADDIM


def _ssd_kernel(n_ref, wz_ref, wxbc_ref, wdt_ref, wdtt_ref, cw_ref, cb_ref, dtb_ref, aneg_ref,
                dtbt_ref, anegt_ref, dsk_ref, gn_ref, y_ref,
                st_ref, hist_ref, edge_ref, xbc_ref, zs_ref, yacc_ref):
    @pl.when(pl.program_id(1) == 0)
    def _():
        st_ref[...] = jnp.zeros_like(st_ref)
        hist_ref[...] = jnp.zeros_like(hist_ref)

    n = n_ref[...].astype(BF16)
    r = lax.broadcasted_iota(jnp.int32, (SEQ_TILE, SEQ_TILE), 0)
    c = lax.broadcasted_iota(jnp.int32, (SEQ_TILE, SEQ_TILE), 1)
    same_chunk = c // CHUNK == r // CHUNK
    tril_chunk = ((c <= r) & same_chunk).astype(BF16)
    triu_chunk = ((r <= c) & same_chunk).astype(BF16)
    e_r = lax.broadcasted_iota(jnp.int32, (_PAIR_W, COL_BLOCK), 0)
    e_c = lax.broadcasted_iota(jnp.int32, (_PAIR_W, COL_BLOCK), 1)

    dt_c = _softplus(_dot(n, wdt_ref[...]) + dtb_ref[...])
    acs_c = _dot_sel(tril_chunk, dt_c * aneg_ref[...])
    dt_parts = jnp.concatenate(_split_bf16(dt_c, 2), axis=1)
    acs_parts = jnp.concatenate(_split_bf16(acs_c, 3), axis=1)
    a_t = _softplus(_dot_nt(wdtt_ref[...], n) + dtbt_ref[...]) * anegt_ref[...]
    acs_t = _dot_sel_r(a_t, triu_chunk, 3)

    first_expand = (M_CONV_DIM - M_DINNER) // COL_BLOCK
    dt_blocks, acs_blocks = [], []
    for cb_i in range(M_CONV_DIM // COL_BLOCK):
        cols = slice(cb_i * COL_BLOCK, (cb_i + 1) * COL_BLOCK)
        x = _dot(n, wxbc_ref[:, cols])
        cw = cw_ref[:, cols]
        cb = cb_ref[:, cols]
        x1 = pltpu.roll(x, 1, 0)
        near = x * cw[3:4] + x1 * cw[2:3]
        far = x * cw[1:2] + x1 * cw[0:1]
        xbc_ref[:, cols] = _silu(near + pltpu.roll(far, 2, 0) + cb)
        edge_ref[0:8, cols] = hist_ref[:, cols]
        edge_ref[8:16, cols] = x[0:8]
        top = _bcast_rows(cb, 8)
        for kk in range(M_CONV):
            top = top + edge_ref[8 - (M_CONV - 1) + kk:16 - (M_CONV - 1) + kk, cols] * cw[kk:kk + 1]
        xbc_ref[0:8, cols] = _silu(top)
        hist_ref[:, cols] = x[SEQ_TILE - 8:SEQ_TILE]
        if cb_i >= first_expand:
            expand = ((e_c + (cb_i - first_expand) * COL_BLOCK) // M_HEADDIM == e_r).astype(BF16)
            dt_blocks.append(_dot(dt_parts, jnp.concatenate([expand] * 2, axis=0)))
            acs_blocks.append(_dot(acs_parts, jnp.concatenate([expand] * 3, axis=0)))
    dt_e = jnp.concatenate(dt_blocks, axis=1)
    acs_e = jnp.concatenate(acs_blocks, axis=1)

    lane_left = lax.broadcasted_iota(jnp.int32, (_PAIRS, _PAIR_W), 1) < CHUNK
    acs_pair = []
    for vc in range(SEQ_TILE // _PAIR_W):
        ev = acs_t[0:_PAIRS, vc * _PAIR_W:(vc + 1) * _PAIR_W]
        od = acs_t[_PAIRS:2 * _PAIRS, vc * _PAIR_W:(vc + 1) * _PAIR_W]
        acs_pair.append(jnp.where(lane_left, ev, pltpu.roll(od, CHUNK, 1)))
        acs_pair.append(jnp.where(lane_left, pltpu.roll(ev, CHUNK, 1), od))

    xdt = xbc_ref[:, 0:M_DINNER] * dt_e
    w_start = jnp.exp(acs_e)

    t_i = lax.broadcasted_iota(jnp.int32, (CHUNK, _PAIR_W), 0)
    l_i = lax.broadcasted_iota(jnp.int32, (CHUNK, _PAIR_W), 1)
    causal2 = (l_i % CHUNK) <= t_i
    left = l_i < M_HEADDIM

    row_chunk = lax.broadcasted_iota(jnp.int32, (SEQ_TILE, M_STATE), 0) // CHUNK
    acs_last = [acs_e[(ci + 1) * CHUNK - 1:(ci + 1) * CHUNK] for ci in range(N_CHUNKS)]
    acs_last_rows = jnp.concatenate([_bcast_rows(a, CHUNK) for a in acs_last], axis=0)
    xw = (xdt * jnp.exp(acs_last_rows - acs_e)).astype(BF16)
    xdt_b = xdt.astype(BF16)

    def by_chunk(v):
        zero = jnp.zeros_like(v)
        return jnp.concatenate([jnp.where(row_chunk == ci, v, zero) for ci in range(N_CHUNKS)], axis=1)

    for g in range(M_GROUPS):
        gl = slice(g * M_GROUP_W, (g + 1) * M_GROUP_W)
        b_g = xbc_ref[:, M_DINNER + g * M_STATE:M_DINNER + (g + 1) * M_STATE].astype(BF16)
        c_g = xbc_ref[:, M_DINNER + (M_GROUPS + g) * M_STATE:
                      M_DINNER + (M_GROUPS + g + 1) * M_STATE].astype(BF16)
        zs_ref[:, gl] = _silu(_dot(n, wz_ref[:, gl]))
        b_dup = jnp.concatenate([b_g[ci * CHUNK:(ci + 1) * CHUNK] for ci in range(N_CHUNKS) for _ in (0, 1)],
                                axis=0)
        cb_all = _dot_nt(c_g, b_dup)
        for ci in range(N_CHUNKS):
            rows = slice(ci * CHUNK, (ci + 1) * CHUNK)
            cb2 = cb_all[rows, ci * _PAIR_W:(ci + 1) * _PAIR_W]
            for pp in range(M_GROUP_W // _PAIR_W):
                p = g * (M_GROUP_W // _PAIR_W) + pp
                pl_ = slice(p * _PAIR_W, (p + 1) * _PAIR_W)
                seg = acs_e[rows, pl_] - acs_pair[ci][p:p + 1, :]
                decay = jnp.exp(jnp.where(causal2, seg, -jnp.inf))
                m = (cb2 * decay).astype(BF16)
                xp = xdt_b[rows, pl_]
                zero = jnp.zeros_like(xp)
                rhs = jnp.concatenate([jnp.where(left, xp, zero), jnp.where(left, zero, xp)], axis=0)
                yacc_ref[rows, pl_] = _dot(m, rhs)
        incr = _dot_tn(by_chunk(b_g), xw[:, gl])
        st = st_ref[g]
        starts = []
        for ci in range(N_CHUNKS):
            starts.append(st.astype(BF16))
            st = st * jnp.exp(acs_last[ci][:, gl]) + incr[ci * M_STATE:(ci + 1) * M_STATE]
        st_ref[g] = st
        y_inter = _dot(by_chunk(c_g), jnp.concatenate(starts, axis=0))
        yacc_ref[:, gl] += y_inter * w_start[:, gl]

    gn = gn_ref[...]
    for g in range(M_GROUPS):
        gl = slice(g * M_GROUP_W, (g + 1) * M_GROUP_W)
        y = (yacc_ref[:, gl] + dsk_ref[:, gl] * xbc_ref[:, gl]) * zs_ref[:, gl]
        y_ref[:, gl] = _rms(y, gn[:, gl]).astype(BF16)


def _ssd(n, wz, wxbc, wdt, wdtt, cw, cb, dtb, aneg, dtbt, anegt, dsk, gn, batch, seq):
    tiles = seq // SEQ_TILE
    seq_spec = pl.BlockSpec((SEQ_TILE, D_MODEL), lambda b, j: (b * tiles + j, 0))
    params = (wz, wxbc, wdt, wdtt, cw, cb, dtb, aneg, dtbt, anegt, dsk, gn)
    return pl.pallas_call(
        _ssd_kernel,
        grid=(batch, tiles),
        in_specs=[seq_spec] + [_resident(a.shape) for a in params],
        out_specs=pl.BlockSpec((SEQ_TILE, M_DINNER), lambda b, j: (b * tiles + j, 0)),
        out_shape=jax.ShapeDtypeStruct((batch * seq, M_DINNER), BF16),
        scratch_shapes=[
            pltpu.VMEM((M_GROUPS, M_STATE, M_GROUP_W), F32),
            pltpu.VMEM((8, M_CONV_DIM), F32),
            pltpu.VMEM((16, M_CONV_DIM), F32),
            pltpu.VMEM((SEQ_TILE, M_CONV_DIM), F32),
            pltpu.VMEM((SEQ_TILE, M_DINNER), F32),
            pltpu.VMEM((SEQ_TILE, M_DINNER), F32),
        ],
        compiler_params=pltpu.CompilerParams(dimension_semantics=("arbitrary", "arbitrary"),
                                             vmem_limit_bytes=VMEM_LIMIT),
        name="ssd",
    )(n, *params)


def _merge_kernel(h_ref, n_ref, ohg_ref, y_ref, wbr_ref, whg_ref, wssm_ref, wout_ref, o_ref):
    br = _sigmoid(_dot(n_ref[...].astype(BF16), wbr_ref[...]))
    mixed = (br[:, 0:D_MODEL] * _dot(ohg_ref[...], whg_ref[...])
             + br[:, D_MODEL:2 * D_MODEL] * _dot(y_ref[...], wssm_ref[...]))
    o_ref[...] = h_ref[...] + _dot(mixed.astype(BF16), wout_ref[...])


def _merge(h, n, ohg, y, wbr, whg, wssm, wout):
    t = h.shape[0]
    return pl.pallas_call(
        _merge_kernel,
        grid=(t // ROW_TILE,),
        in_specs=[_row_spec(D_MODEL), _row_spec(D_MODEL), _row_spec(HG_VDIM), _row_spec(M_DINNER),
                  _resident(wbr.shape), _resident(whg.shape), _resident(wssm.shape),
                  _resident(wout.shape)],
        out_specs=_row_spec(D_MODEL),
        out_shape=jax.ShapeDtypeStruct((t, D_MODEL), F32),
        compiler_params=pltpu.CompilerParams(dimension_semantics=("arbitrary",),
                                             vmem_limit_bytes=VMEM_LIMIT),
        name="merge",
    )(h, n, ohg, y, wbr, whg, wssm, wout)


def kernel(x, p, ffn1_norm, ffn1_w13, ffn1_w2, mix_norm, w_in, conv_w, conv_b, dt_bias, a_log,
           d_skip, ssm_norm, hg_lb, hg_norm, w_hg_out, w_ssm_out, w_out, ffn2_norm, ffn2_w13,
           ffn2_w2, ple_norm, w_ple_gate, w_ple_proj, final_norm):
    batch, seq, _ = x.shape
    assert ffn1_w13.shape[0] == 1 and hg_lb.shape[0] == 2, "single-layer block"
    assert seq % SEQ_TILE == 0 and (batch * seq) % ROW_TILE == 0
    t = batch * seq
    row = lambda v: v.reshape(1, -1).astype(F32)
    col = lambda v: v.reshape(-1, 1).astype(F32)
    expand = lambda v: jnp.repeat(v, M_HEADDIM, axis=-1)
    pad_heads = lambda v: jnp.pad(v, [(0, 0)] * (v.ndim - 1) + [(0, _PAIR_W - M_HEADS)])
    even_odd = lambda v: jnp.concatenate([v[..., 0::2], v[..., 1::2]], axis=-1)

    sizes = (HG_KDIM, HG_KDIM, HG_VDIM, HG_VDIM, M_DINNER, M_CONV_DIM, M_HEADS, 2 * D_MODEL)
    offs = [0]
    for s in sizes:
        offs.append(offs[-1] + s)
    w = w_in[0]
    w_hg = w[:, offs[0]:offs[4]].astype(BF16)
    w_z = w[:, offs[4]:offs[5]].astype(BF16)
    w_xbc = w[:, offs[5]:offs[6]].astype(BF16)
    w_dt = w[:, offs[6]:offs[7]].astype(BF16)
    w_br = w[:, offs[7]:offs[8]].astype(BF16)
    a_neg = -jnp.exp(a_log[0].astype(F32))
    dtb = dt_bias[0].astype(F32)

    x2 = x.reshape(t, D_MODEL)
    h1, n = _ffn1(x2, row(ffn1_norm[0]), ffn1_w13[0].astype(BF16), ffn1_w2[0].astype(BF16),
                  row(mix_norm[0]))
    o_hg = _hgrn(n, w_hg, hg_lb.astype(F32), row(hg_norm[0]), batch, seq)
    y = _ssd(n, w_z, w_xbc, pad_heads(w_dt), even_odd(w_dt).T,
             conv_w[0].astype(F32), row(conv_b[0]), row(pad_heads(dtb)), row(pad_heads(a_neg)),
             col(even_odd(dtb)), col(even_odd(a_neg)),
             row(expand(d_skip[0].astype(F32))), row(ssm_norm[0]), batch, seq)
    h2 = _merge(h1, n, o_hg, y, w_br, w_hg_out[0].astype(BF16), w_ssm_out[0].astype(BF16),
                w_out[0].astype(BF16))
    out = _ffn2(h2, row(ffn2_norm[0]), ffn2_w13[0].astype(BF16), ffn2_w2[0].astype(BF16),
                p[0].reshape(t, PLE_DIM), row(ple_norm[0]), w_ple_gate[0].astype(BF16),
                w_ple_proj[0].astype(BF16), row(final_norm))
    return out.reshape(batch, seq, D_MODEL)
```

```python
import jax
import jax.numpy as jnp
from jax import lax
from jax.experimental import pallas as pl
from jax.experimental.pallas import tpu as pltpu

F32 = jnp.float32
BF16 = jnp.bfloat16

D_MODEL = 1024
D_FF = 2816
PLE_DIM = 256
EPS = 1e-6
HG_HEADS = 8
HG_DK = 128
HG_DV = 128
HG_KDIM = HG_HEADS * HG_DK
HG_VDIM = HG_HEADS * HG_DV
HG_BLOCK = 16
M_DINNER = 2048
M_HEADDIM = 64
M_HEADS = 32
M_STATE = 128
M_GROUPS = 4
M_CONV = 4
M_GROUP_W = M_DINNER // M_GROUPS
M_CONV_DIM = M_DINNER + 2 * M_GROUPS * M_STATE
CHUNK = 64

ROW_TILE = 512
SEQ_TILE = 256
N_CHUNKS = SEQ_TILE // CHUNK
COL_BLOCK = 512
VMEM_LIMIT = 56 * 1024 * 1024


def _dot(a, b):
    return jnp.dot(a, b, preferred_element_type=F32)


def _dot_nt(a, b):
    return lax.dot_general(a, b, (((1,), (1,)), ((), ())), preferred_element_type=F32)


def _dot_tn(a, b):
    return lax.dot_general(a, b, (((0,), (0,)), ((), ())), preferred_element_type=F32)


def _split_bf16(x, parts):
    out, rem = [], x
    for _ in range(parts):
        p = rem.astype(BF16)
        out.append(p)
        rem = rem - p.astype(F32)
    return out


def _dot_sel(sel, x, parts=3):
    return _dot(jnp.concatenate([sel] * parts, axis=1), jnp.concatenate(_split_bf16(x, parts), axis=0))


def _dot_sel_r(x, sel, parts=3):
    return _dot(jnp.concatenate(_split_bf16(x, parts), axis=1), jnp.concatenate([sel] * parts, axis=0))


def _rms(x, g):
    ms = jnp.mean(x * x, axis=-1, keepdims=True)
    return x * lax.rsqrt(ms + EPS) * g


def _sigmoid(x):
    return 0.5 * jnp.tanh(0.5 * x) + 0.5


def _silu(x):
    h = 0.5 * x
    return h + h * jnp.tanh(h)


def _softplus(x):
    return jnp.maximum(x, 0.0) + jnp.log(1.0 + jnp.exp(-jnp.abs(x)))


def _bcast_rows(row, rows):
    return jnp.broadcast_to(row, (rows, row.shape[-1]))


def _resident(shape):
    nd = len(shape)
    return pl.BlockSpec(shape, lambda *_: (0,) * nd, pipeline_mode=pl.Buffered(1))


def _swiglu_residual(x, g_ref, w13_ref, w2_ref):
    n = _rms(x, g_ref[...]).astype(BF16)
    gate = _dot(n, w13_ref[:, 0:D_FF])
    up = _dot(n, w13_ref[:, D_FF:2 * D_FF])
    act = (_silu(gate) * up).astype(BF16)
    return x + 0.5 * _dot(act, w2_ref[...])


def _ffn1_kernel(x_ref, g_ref, w13_ref, w2_ref, mixg_ref, h_ref, n_ref):
    h = _swiglu_residual(x_ref[...], g_ref, w13_ref, w2_ref)
    h_ref[...] = h
    n_ref[...] = _rms(h, mixg_ref[...])


def _ffn2_kernel(x_ref, g_ref, w13_ref, w2_ref, p_ref, pleg_ref, wpg_ref, wpp_ref, fing_ref, o_ref):
    h = _swiglu_residual(x_ref[...], g_ref, w13_ref, w2_ref)
    gate = _sigmoid(_dot(_rms(h, pleg_ref[...]).astype(BF16), wpg_ref[...]))
    emb = _dot(p_ref[...].astype(BF16), wpp_ref[...])
    h = h + gate * emb
    o_ref[...] = _rms(h, fing_ref[...])


def _row_spec(width):
    return pl.BlockSpec((ROW_TILE, width), lambda i: (i, 0))


def _ffn1(x, g, w13, w2, mixg):
    t = x.shape[0]
    return pl.pallas_call(
        _ffn1_kernel,
        grid=(t // ROW_TILE,),
        in_specs=[_row_spec(D_MODEL), _resident((1, D_MODEL)), _resident(w13.shape),
                  _resident(w2.shape), _resident((1, D_MODEL))],
        out_specs=[_row_spec(D_MODEL), _row_spec(D_MODEL)],
        out_shape=[jax.ShapeDtypeStruct((t, D_MODEL), F32),
                   jax.ShapeDtypeStruct((t, D_MODEL), F32)],
        compiler_params=pltpu.CompilerParams(dimension_semantics=("arbitrary",),
                                             vmem_limit_bytes=VMEM_LIMIT),
        name="ffn1",
    )(x, g, w13, w2, mixg)


def _ffn2(x, g, w13, w2, p, pleg, wpg, wpp, fing):
    t = x.shape[0]
    return pl.pallas_call(
        _ffn2_kernel,
        grid=(t // ROW_TILE,),
        in_specs=[_row_spec(D_MODEL), _resident((1, D_MODEL)), _resident(w13.shape),
                  _resident(w2.shape), _row_spec(PLE_DIM), _resident((1, D_MODEL)),
                  _resident(wpg.shape), _resident(wpp.shape), _resident((1, D_MODEL))],
        out_specs=_row_spec(D_MODEL),
        out_shape=jax.ShapeDtypeStruct((t, D_MODEL), F32),
        compiler_params=pltpu.CompilerParams(dimension_semantics=("arbitrary",),
                                             vmem_limit_bytes=VMEM_LIMIT),
        name="ffn2",
    )(x, g, w13, w2, p, pleg, wpg, wpp, fing)


def _hgrn_kernel(n_ref, w_ref, lb_ref, gn_ref, o_ref,
                 st_ref, q_ref, k_ref, gl_ref, v_ref, gate_ref, oacc_ref):
    @pl.when(pl.program_id(1) == 0)
    def _():
        st_ref[...] = jnp.zeros_like(st_ref)

    n = n_ref[...].astype(BF16)
    a = lb_ref[...]
    e = jnp.exp(a - jnp.max(a, axis=0, keepdims=True))
    lb = e[0:1] / jnp.sum(e, axis=0, keepdims=True)

    r = lax.broadcasted_iota(jnp.int32, (CHUNK, CHUNK), 0)
    c = lax.broadcasted_iota(jnp.int32, (CHUNK, CHUNK), 1)
    tril_blk = ((c <= r) & (c // HG_BLOCK == r // HG_BLOCK)).astype(BF16)
    f_raws = [_dot(n, w_ref[:, HG_KDIM + cb * COL_BLOCK:HG_KDIM + (cb + 1) * COL_BLOCK])
              for cb in range(HG_KDIM // COL_BLOCK)]
    q_ref[...] = _dot(n, w_ref[:, 0:HG_KDIM]) * (HG_DK ** -0.5)
    v_ref[...] = _dot(n, w_ref[:, 2 * HG_KDIM:2 * HG_KDIM + HG_VDIM]).astype(BF16)
    for cb, f_raw in enumerate(f_raws):
        cols = slice(cb * COL_BLOCK, (cb + 1) * COL_BLOCK)
        f = lb[:, cols] + (1.0 - lb[:, cols]) * _sigmoid(f_raw)
        k_ref[:, cols] = 1.0 - f
        logf = jnp.log(f)
        for ci in range(N_CHUNKS):
            rows = slice(ci * CHUNK, (ci + 1) * CHUNK)
            gl_ref[rows, cols] = _dot_sel(tril_blk, logf[rows])

    t_i = lax.broadcasted_iota(jnp.int32, (CHUNK, 3 * CHUNK), 0)
    c_i = lax.broadcasted_iota(jnp.int32, (CHUNK, 3 * CHUNK), 1)
    s_i, grp = c_i % CHUNK, c_i // CHUNK
    bt, bs = t_i // HG_BLOCK, s_i // HG_BLOCK
    m_all = (((grp == 0) & (bt == bs) & (s_i <= t_i))
             | ((grp == 1) & (bt % 2 == 1) & (bs == bt - 1))
             | ((grp == 2) & (bt >= 2) & (bs < 2)))
    zero_blk = jnp.zeros((CHUNK, HG_DK), BF16)
    ones_blk = jnp.ones((HG_BLOCK, HG_KDIM), F32)
    gate_cols = HG_VDIM // N_CHUNKS

    atts = {}
    prepped = []
    for ci in range(N_CHUNKS):
        rows = slice(ci * CHUNK, (ci + 1) * CHUNK)
        q, k, gl, v = q_ref[rows, :], k_ref[rows, :], gl_ref[rows, :], v_ref[rows, :]
        tot = [gl[HG_BLOCK * i + HG_BLOCK - 1:HG_BLOCK * (i + 1), :] for i in range(4)]
        e_t = [jnp.exp(t) for t in tot]
        e01, e23 = e_t[0] * e_t[1], e_t[2] * e_t[3]
        dec = e01 * e23
        tot_rows = jnp.concatenate([_bcast_rows(t, HG_BLOCK) for t in tot], axis=0)
        qd = q * jnp.exp(gl)
        ki = k * jnp.exp(-gl)
        ke = k * jnp.exp(tot_rows - gl)
        qd32 = qd * jnp.concatenate(
            [ones_blk, _bcast_rows(e_t[0], HG_BLOCK), ones_blk, _bcast_rows(e_t[2], HG_BLOCK)], axis=0)
        ke32 = ke * jnp.concatenate(
            [_bcast_rows(e_t[1], HG_BLOCK), ones_blk, _bcast_rows(e_t[3], HG_BLOCK), ones_blk], axis=0)
        qd64 = qd32 * jnp.concatenate(
            [ones_blk, ones_blk, _bcast_rows(e01, HG_BLOCK), _bcast_rows(e01, HG_BLOCK)], axis=0)
        ke64 = ke32 * jnp.concatenate(
            [_bcast_rows(e23, HG_BLOCK), _bcast_rows(e23, HG_BLOCK), ones_blk, ones_blk], axis=0)
        qd, ki, ke, qd32, ke32, qd64, ke64 = (
            t.astype(BF16) for t in (qd, ki, ke, qd32, ke32, qd64, ke64))
        prepped.append((v, dec, qd64, ke64))
        g0 = 2 * HG_KDIM + HG_VDIM + ci * gate_cols
        gate_ref[:, ci * gate_cols:(ci + 1) * gate_cols] = _silu(_dot(n, w_ref[:, g0:g0 + gate_cols]))
        for h in range(HG_HEADS):
            kl = slice(h * HG_DK, (h + 1) * HG_DK)
            lhs = jnp.concatenate([qd[:, kl], qd32[:, kl]], axis=1)
            rhs_t = jnp.concatenate([jnp.concatenate([ki[:, kl], zero_blk], axis=1),
                                     jnp.concatenate([ke[:, kl], zero_blk], axis=1),
                                     jnp.concatenate([zero_blk, ke32[:, kl]], axis=1)], axis=0)
            atts[ci, h] = jnp.where(m_all, _dot_nt(lhs, rhs_t), 0.0).astype(BF16)
    gn = gn_ref[...]
    for ci in range(N_CHUNKS):
        rows = slice(ci * CHUNK, (ci + 1) * CHUNK)
        v, dec, qd64, ke64 = prepped[ci]
        for h in range(HG_HEADS):
            kl = slice(h * HG_DK, (h + 1) * HG_DK)
            vl = slice(h * HG_DV, (h + 1) * HG_DV)
            st = st_ref[h]
            v3 = jnp.concatenate([v[:, vl]] * 3, axis=0)
            oacc_ref[rows, vl] = _dot(atts[ci, h], v3) + _dot_nt(qd64[:, kl], st.astype(BF16))
            st_ref[h] = st * dec[:, kl] + _dot_tn(v[:, vl], ke64[:, kl])
        for h in range(HG_HEADS):
            vl = slice(h * HG_DV, (h + 1) * HG_DV)
            o_ref[rows, vl] = (_rms(oacc_ref[rows, vl], gn[:, vl]) * gate_ref[rows, vl]).astype(BF16)


def _hgrn(n, w, hg_lb, hg_norm, batch, seq):
    tiles = seq // SEQ_TILE
    seq_spec = pl.BlockSpec((SEQ_TILE, D_MODEL), lambda b, j: (b * tiles + j, 0))
    return pl.pallas_call(
        _hgrn_kernel,
        grid=(batch, tiles),
        in_specs=[seq_spec, _resident(w.shape), _resident(hg_lb.shape), _resident((1, HG_VDIM))],
        out_specs=pl.BlockSpec((SEQ_TILE, HG_VDIM), lambda b, j: (b * tiles + j, 0)),
        out_shape=jax.ShapeDtypeStruct((batch * seq, HG_VDIM), BF16),
        scratch_shapes=[
            pltpu.VMEM((HG_HEADS, HG_DV, HG_DK), F32),
            pltpu.VMEM((SEQ_TILE, HG_KDIM), F32),
            pltpu.VMEM((SEQ_TILE, HG_KDIM), F32),
            pltpu.VMEM((SEQ_TILE, HG_KDIM), F32),
            pltpu.VMEM((SEQ_TILE, HG_VDIM), BF16),
            pltpu.VMEM((SEQ_TILE, HG_VDIM), F32),
            pltpu.VMEM((SEQ_TILE, HG_VDIM), F32),
        ],
        compiler_params=pltpu.CompilerParams(dimension_semantics=("arbitrary", "arbitrary"),
                                             vmem_limit_bytes=VMEM_LIMIT),
        name="hgrn2",
    )(n, w, hg_lb, hg_norm)


_PAIRS = M_HEADS // 2
_PAIR_W = 2 * M_HEADDIM


def _ssd_kernel(n_ref, wz_ref, wxbc_ref, wdt_ref, wdtt_ref, cw_ref, cb_ref, dtb_ref, aneg_ref,
                dtbt_ref, anegt_ref, dsk_ref, gn_ref, y_ref,
                st_ref, hist_ref, edge_ref, xbc_ref, zs_ref, yacc_ref):
    @pl.when(pl.program_id(1) == 0)
    def _():
        st_ref[...] = jnp.zeros_like(st_ref)
        hist_ref[...] = jnp.zeros_like(hist_ref)

    n = n_ref[...].astype(BF16)
    r = lax.broadcasted_iota(jnp.int32, (SEQ_TILE, SEQ_TILE), 0)
    c = lax.broadcasted_iota(jnp.int32, (SEQ_TILE, SEQ_TILE), 1)
    same_chunk = c // CHUNK == r // CHUNK
    tril_chunk = ((c <= r) & same_chunk).astype(BF16)
    triu_chunk = ((r <= c) & same_chunk).astype(BF16)
    e_r = lax.broadcasted_iota(jnp.int32, (_PAIR_W, COL_BLOCK), 0)
    e_c = lax.broadcasted_iota(jnp.int32, (_PAIR_W, COL_BLOCK), 1)

    dt_c = _softplus(_dot(n, wdt_ref[...]) + dtb_ref[...])
    a_c = dt_c * aneg_ref[...]
    acs_c = jnp.concatenate([_dot_sel(tril_chunk[0:CHUNK, 0:CHUNK], a_c[ci * CHUNK:(ci + 1) * CHUNK])
                             for ci in range(N_CHUNKS)], axis=0)
    dt_parts = jnp.concatenate(_split_bf16(dt_c, 2), axis=1)
    acs_parts = jnp.concatenate(_split_bf16(acs_c, 3), axis=1)
    a_t = _softplus(_dot_nt(wdtt_ref[...], n) + dtbt_ref[...]) * anegt_ref[...]
    acs_t = _dot_sel_r(a_t, triu_chunk, 3)

    first_expand = (M_CONV_DIM - M_DINNER) // COL_BLOCK
    dt_blocks, acs_blocks = [], []
    for cb_i in range(M_CONV_DIM // COL_BLOCK):
        cols = slice(cb_i * COL_BLOCK, (cb_i + 1) * COL_BLOCK)
        x = _dot(n, wxbc_ref[:, cols])
        cw = cw_ref[:, cols]
        cb = cb_ref[:, cols]
        x1 = pltpu.roll(x, 1, 0)
        near = x * cw[3:4] + x1 * cw[2:3]
        far = x * cw[1:2] + x1 * cw[0:1]
        xbc_ref[:, cols] = _silu(near + pltpu.roll(far, 2, 0) + cb)
        edge_ref[0:8, cols] = hist_ref[:, cols]
        edge_ref[8:16, cols] = x[0:8]
        top = _bcast_rows(cb, 8)
        for kk in range(M_CONV):
            top = top + edge_ref[8 - (M_CONV - 1) + kk:16 - (M_CONV - 1) + kk, cols] * cw[kk:kk + 1]
        xbc_ref[0:8, cols] = _silu(top)
        hist_ref[:, cols] = x[SEQ_TILE - 8:SEQ_TILE]
        if cb_i >= first_expand:
            expand = ((e_c + (cb_i - first_expand) * COL_BLOCK) // M_HEADDIM == e_r).astype(BF16)
            dt_blocks.append(_dot(dt_parts, jnp.concatenate([expand] * 2, axis=0)))
            acs_blocks.append(_dot(acs_parts, jnp.concatenate([expand] * 3, axis=0)))
    dt_e = jnp.concatenate(dt_blocks, axis=1)
    acs_e = jnp.concatenate(acs_blocks, axis=1)

    lane_left = lax.broadcasted_iota(jnp.int32, (_PAIRS, _PAIR_W), 1) < CHUNK
    acs_pair = []
    for vc in range(SEQ_TILE // _PAIR_W):
        ev = acs_t[0:_PAIRS, vc * _PAIR_W:(vc + 1) * _PAIR_W]
        od = acs_t[_PAIRS:2 * _PAIRS, vc * _PAIR_W:(vc + 1) * _PAIR_W]
        acs_pair.append(jnp.where(lane_left, ev, pltpu.roll(od, CHUNK, 1)))
        acs_pair.append(jnp.where(lane_left, pltpu.roll(ev, CHUNK, 1), od))

    xdt = xbc_ref[:, 0:M_DINNER] * dt_e
    w_start = jnp.exp(acs_e)

    t_i = lax.broadcasted_iota(jnp.int32, (CHUNK, _PAIR_W), 0)
    l_i = lax.broadcasted_iota(jnp.int32, (CHUNK, _PAIR_W), 1)
    causal2 = (l_i % CHUNK) <= t_i
    left = l_i < M_HEADDIM

    row_chunk = lax.broadcasted_iota(jnp.int32, (SEQ_TILE, M_STATE), 0) // CHUNK
    acs_last = [acs_e[(ci + 1) * CHUNK - 1:(ci + 1) * CHUNK] for ci in range(N_CHUNKS)]
    acs_last_rows = jnp.concatenate([_bcast_rows(a, CHUNK) for a in acs_last], axis=0)
    xw = (xdt * jnp.exp(acs_last_rows - acs_e)).astype(BF16)
    xdt_b = xdt.astype(BF16)

    def by_chunk(v):
        zero = jnp.zeros_like(v)
        return jnp.concatenate([jnp.where(row_chunk == ci, v, zero) for ci in range(N_CHUNKS)], axis=1)

    for g in range(M_GROUPS):
        gl = slice(g * M_GROUP_W, (g + 1) * M_GROUP_W)
        b_g = xbc_ref[:, M_DINNER + g * M_STATE:M_DINNER + (g + 1) * M_STATE].astype(BF16)
        c_g = xbc_ref[:, M_DINNER + (M_GROUPS + g) * M_STATE:
                      M_DINNER + (M_GROUPS + g + 1) * M_STATE].astype(BF16)
        zs_ref[:, gl] = _silu(_dot(n, wz_ref[:, gl]))
        b_dup = jnp.concatenate([b_g[ci * CHUNK:(ci + 1) * CHUNK] for ci in range(N_CHUNKS) for _ in (0, 1)],
                                axis=0)
        cb_all = _dot_nt(c_g, b_dup)
        for ci in range(N_CHUNKS):
            rows = slice(ci * CHUNK, (ci + 1) * CHUNK)
            cb2 = cb_all[rows, ci * _PAIR_W:(ci + 1) * _PAIR_W]
            for pp in range(M_GROUP_W // _PAIR_W):
                p = g * (M_GROUP_W // _PAIR_W) + pp
                pl_ = slice(p * _PAIR_W, (p + 1) * _PAIR_W)
                seg = acs_e[rows, pl_] - acs_pair[ci][p:p + 1, :]
                decay = jnp.exp(jnp.where(causal2, seg, -jnp.inf))
                m = (cb2 * decay).astype(BF16)
                xp = xdt_b[rows, pl_]
                zero = jnp.zeros_like(xp)
                rhs = jnp.concatenate([jnp.where(left, xp, zero), jnp.where(left, zero, xp)], axis=0)
                yacc_ref[rows, pl_] = _dot(m, rhs)
        incr = _dot_tn(by_chunk(b_g), xw[:, gl])
        st = st_ref[g]
        starts = []
        for ci in range(N_CHUNKS):
            starts.append(st.astype(BF16))
            st = st * jnp.exp(acs_last[ci][:, gl]) + incr[ci * M_STATE:(ci + 1) * M_STATE]
        st_ref[g] = st
        y_inter = _dot(by_chunk(c_g), jnp.concatenate(starts, axis=0))
        y = (yacc_ref[:, gl] + y_inter * w_start[:, gl] + dsk_ref[:, gl] * xbc_ref[:, gl]) * zs_ref[:, gl]
        y_ref[:, gl] = _rms(y, gn_ref[:, gl]).astype(BF16)


def _ssd(n, wz, wxbc, wdt, wdtt, cw, cb, dtb, aneg, dtbt, anegt, dsk, gn, batch, seq):
    tiles = seq // SEQ_TILE
    seq_spec = pl.BlockSpec((SEQ_TILE, D_MODEL), lambda b, j: (b * tiles + j, 0))
    params = (wz, wxbc, wdt, wdtt, cw, cb, dtb, aneg, dtbt, anegt, dsk, gn)
    return pl.pallas_call(
        _ssd_kernel,
        grid=(batch, tiles),
        in_specs=[seq_spec] + [_resident(a.shape) for a in params],
        out_specs=pl.BlockSpec((SEQ_TILE, M_DINNER), lambda b, j: (b * tiles + j, 0)),
        out_shape=jax.ShapeDtypeStruct((batch * seq, M_DINNER), BF16),
        scratch_shapes=[
            pltpu.VMEM((M_GROUPS, M_STATE, M_GROUP_W), F32),
            pltpu.VMEM((8, M_CONV_DIM), F32),
            pltpu.VMEM((16, M_CONV_DIM), F32),
            pltpu.VMEM((SEQ_TILE, M_CONV_DIM), F32),
            pltpu.VMEM((SEQ_TILE, M_DINNER), F32),
            pltpu.VMEM((SEQ_TILE, M_DINNER), F32),
        ],
        compiler_params=pltpu.CompilerParams(dimension_semantics=("arbitrary", "arbitrary"),
                                             vmem_limit_bytes=VMEM_LIMIT),
        name="ssd",
    )(n, *params)


def _merge_kernel(h_ref, n_ref, ohg_ref, y_ref, wbr_ref, whg_ref, wssm_ref, wout_ref, o_ref):
    br = _sigmoid(_dot(n_ref[...].astype(BF16), wbr_ref[...]))
    mixed = (br[:, 0:D_MODEL] * _dot(ohg_ref[...], whg_ref[...])
             + br[:, D_MODEL:2 * D_MODEL] * _dot(y_ref[...], wssm_ref[...]))
    o_ref[...] = h_ref[...] + _dot(mixed.astype(BF16), wout_ref[...])


def _merge(h, n, ohg, y, wbr, whg, wssm, wout):
    t = h.shape[0]
    return pl.pallas_call(
        _merge_kernel,
        grid=(t // ROW_TILE,),
        in_specs=[_row_spec(D_MODEL), _row_spec(D_MODEL), _row_spec(HG_VDIM), _row_spec(M_DINNER),
                  _resident(wbr.shape), _resident(whg.shape), _resident(wssm.shape),
                  _resident(wout.shape)],
        out_specs=_row_spec(D_MODEL),
        out_shape=jax.ShapeDtypeStruct((t, D_MODEL), F32),
        compiler_params=pltpu.CompilerParams(dimension_semantics=("arbitrary",),
                                             vmem_limit_bytes=VMEM_LIMIT),
        name="merge",
    )(h, n, ohg, y, wbr, whg, wssm, wout)


def kernel(x, p, ffn1_norm, ffn1_w13, ffn1_w2, mix_norm, w_in, conv_w, conv_b, dt_bias, a_log,
           d_skip, ssm_norm, hg_lb, hg_norm, w_hg_out, w_ssm_out, w_out, ffn2_norm, ffn2_w13,
           ffn2_w2, ple_norm, w_ple_gate, w_ple_proj, final_norm):
    batch, seq, _ = x.shape
    assert ffn1_w13.shape[0] == 1 and hg_lb.shape[0] == 2, "single-layer block"
    assert seq % SEQ_TILE == 0 and (batch * seq) % ROW_TILE == 0
    t = batch * seq
    row = lambda v: v.reshape(1, -1).astype(F32)
    col = lambda v: v.reshape(-1, 1).astype(F32)
    expand = lambda v: jnp.repeat(v, M_HEADDIM, axis=-1)
    pad_heads = lambda v: jnp.pad(v, [(0, 0)] * (v.ndim - 1) + [(0, _PAIR_W - M_HEADS)])
    even_odd = lambda v: jnp.concatenate([v[..., 0::2], v[..., 1::2]], axis=-1)

    sizes = (HG_KDIM, HG_KDIM, HG_VDIM, HG_VDIM, M_DINNER, M_CONV_DIM, M_HEADS, 2 * D_MODEL)
    offs = [0]
    for s in sizes:
        offs.append(offs[-1] + s)
    w = w_in[0]
    w_hg = w[:, offs[0]:offs[4]].astype(BF16)
    w_z = w[:, offs[4]:offs[5]].astype(BF16)
    w_xbc = w[:, offs[5]:offs[6]].astype(BF16)
    w_dt = w[:, offs[6]:offs[7]].astype(BF16)
    w_br = w[:, offs[7]:offs[8]].astype(BF16)
    a_neg = -jnp.exp(a_log[0].astype(F32))
    dtb = dt_bias[0].astype(F32)

    x2 = x.reshape(t, D_MODEL)
    h1, n = _ffn1(x2, row(ffn1_norm[0]), ffn1_w13[0].astype(BF16), ffn1_w2[0].astype(BF16),
                  row(mix_norm[0]))
    o_hg = _hgrn(n, w_hg, hg_lb.astype(F32), row(hg_norm[0]), batch, seq)
    y = _ssd(n, w_z, w_xbc, pad_heads(w_dt), even_odd(w_dt).T,
             conv_w[0].astype(F32), row(conv_b[0]), row(pad_heads(dtb)), row(pad_heads(a_neg)),
             col(even_odd(dtb)), col(even_odd(a_neg)),
             row(expand(d_skip[0].astype(F32))), row(ssm_norm[0]), batch, seq)
    h2 = _merge(h1, n, o_hg, y, w_br, w_hg_out[0].astype(BF16), w_ssm_out[0].astype(BF16),
                w_out[0].astype(BF16))
    out = _ffn2(h2, row(ffn2_norm[0]), ffn2_w13[0].astype(BF16), ffn2_w2[0].astype(BF16),
                p[0].reshape(t, PLE_DIM), row(ple_norm[0]), w_ple_gate[0].astype(BF16),
                w_ple_proj[0].astype(BF16), row(final_norm))
    return out.reshape(batch, seq, D_MODEL)
```

```python
import jax
import jax.numpy as jnp
from jax import lax
from jax.experimental import pallas as pl
from jax.experimental.pallas import tpu as pltpu

F32 = jnp.float32
BF16 = jnp.bfloat16

D_MODEL = 1024
D_FF = 2816
PLE_DIM = 256
EPS = 1e-6
HG_HEADS = 8
HG_DK = 128
HG_DV = 128
HG_KDIM = HG_HEADS * HG_DK
HG_VDIM = HG_HEADS * HG_DV
HG_BLOCK = 16
M_DINNER = 2048
M_HEADDIM = 64
M_HEADS = 32
M_STATE = 128
M_GROUPS = 4
M_CONV = 4
M_GROUP_W = M_DINNER // M_GROUPS
M_CONV_DIM = M_DINNER + 2 * M_GROUPS * M_STATE
CHUNK = 64

ROW_TILE = 512
SEQ_TILE = 256
N_CHUNKS = SEQ_TILE // CHUNK
COL_BLOCK = 512
VMEM_LIMIT = 56 * 1024 * 1024


def _dot(a, b):
    return jnp.dot(a, b, preferred_element_type=F32)


def _dot_nt(a, b):
    return lax.dot_general(a, b, (((1,), (1,)), ((), ())), preferred_element_type=F32)


def _dot_tn(a, b):
    return lax.dot_general(a, b, (((0,), (0,)), ((), ())), preferred_element_type=F32)


def _split_bf16(x, parts):
    out, rem = [], x
    for _ in range(parts):
        p = rem.astype(BF16)
        out.append(p)
        rem = rem - p.astype(F32)
    return out


def _dot_sel(sel, x, parts=3):
    return _dot(jnp.concatenate([sel] * parts, axis=1), jnp.concatenate(_split_bf16(x, parts), axis=0))


def _dot_sel_r(x, sel, parts=3):
    return _dot(jnp.concatenate(_split_bf16(x, parts), axis=1), jnp.concatenate([sel] * parts, axis=0))


def _rms(x, g):
    ms = jnp.mean(x * x, axis=-1, keepdims=True)
    return x * lax.rsqrt(ms + EPS) * g


def _sigmoid(x):
    return 0.5 * jnp.tanh(0.5 * x) + 0.5


def _silu(x):
    h = 0.5 * x
    return h + h * jnp.tanh(h)


def _softplus(x):
    return jnp.maximum(x, 0.0) + jnp.log(1.0 + jnp.exp(-jnp.abs(x)))


def _bcast_rows(row, rows):
    return jnp.broadcast_to(row, (rows, row.shape[-1]))


def _resident(shape):
    nd = len(shape)
    return pl.BlockSpec(shape, lambda *_: (0,) * nd, pipeline_mode=pl.Buffered(1))


def _resident_cols(rows, width, block):
    return pl.BlockSpec((rows, width), lambda *_: (0, block), pipeline_mode=pl.Buffered(1))


def _swiglu_residual(x, g_ref, w13_ref, w2_ref):
    n = _rms(x, g_ref[...]).astype(BF16)
    gate = _dot(n, w13_ref[:, 0:D_FF])
    up = _dot(n, w13_ref[:, D_FF:2 * D_FF])
    act = (_silu(gate) * up).astype(BF16)
    return x + 0.5 * _dot(act, w2_ref[...])


def _ffn1_kernel(x_ref, g_ref, w13_ref, w2_ref, mixg_ref, h_ref, n_ref):
    h = _swiglu_residual(x_ref[...], g_ref, w13_ref, w2_ref)
    h_ref[...] = h
    n_ref[...] = _rms(h, mixg_ref[...])


def _ffn2_kernel(x_ref, g_ref, w13_ref, w2_ref, p_ref, pleg_ref, wpg_ref, wpp_ref, fing_ref, o_ref):
    h = _swiglu_residual(x_ref[...], g_ref, w13_ref, w2_ref)
    gate = _sigmoid(_dot(_rms(h, pleg_ref[...]).astype(BF16), wpg_ref[...]))
    emb = _dot(p_ref[...].astype(BF16), wpp_ref[...])
    h = h + gate * emb
    o_ref[...] = _rms(h, fing_ref[...])


def _row_spec(width):
    return pl.BlockSpec((ROW_TILE, width), lambda i: (i, 0))


def _ffn1(x, g, w13, w2, mixg):
    t = x.shape[0]
    return pl.pallas_call(
        _ffn1_kernel,
        grid=(t // ROW_TILE,),
        in_specs=[_row_spec(D_MODEL), _resident((1, D_MODEL)), _resident(w13.shape),
                  _resident(w2.shape), _resident((1, D_MODEL))],
        out_specs=[_row_spec(D_MODEL), _row_spec(D_MODEL)],
        out_shape=[jax.ShapeDtypeStruct((t, D_MODEL), F32),
                   jax.ShapeDtypeStruct((t, D_MODEL), F32)],
        compiler_params=pltpu.CompilerParams(dimension_semantics=("arbitrary",),
                                             vmem_limit_bytes=VMEM_LIMIT),
        name="ffn1",
    )(x, g, w13, w2, mixg)


def _ffn2(x, g, w13, w2, p, pleg, wpg, wpp, fing):
    t = x.shape[0]
    return pl.pallas_call(
        _ffn2_kernel,
        grid=(t // ROW_TILE,),
        in_specs=[_row_spec(D_MODEL), _resident((1, D_MODEL)), _resident(w13.shape),
                  _resident(w2.shape), _row_spec(PLE_DIM), _resident((1, D_MODEL)),
                  _resident(wpg.shape), _resident(wpp.shape), _resident((1, D_MODEL))],
        out_specs=_row_spec(D_MODEL),
        out_shape=jax.ShapeDtypeStruct((t, D_MODEL), F32),
        compiler_params=pltpu.CompilerParams(dimension_semantics=("arbitrary",),
                                             vmem_limit_bytes=VMEM_LIMIT),
        name="ffn2",
    )(x, g, w13, w2, p, pleg, wpg, wpp, fing)


def _hgrn_kernel(n_ref, w_ref, lb_ref, gn_ref, o_ref,
                 st_ref, q_ref, k_ref, gl_ref, v_ref, gate_ref, oacc_ref):
    @pl.when(pl.program_id(1) == 0)
    def _():
        st_ref[...] = jnp.zeros_like(st_ref)

    n = n_ref[...].astype(BF16)
    a = lb_ref[...]
    e = jnp.exp(a - jnp.max(a, axis=0, keepdims=True))
    lb = e[0:1] / jnp.sum(e, axis=0, keepdims=True)

    r = lax.broadcasted_iota(jnp.int32, (CHUNK, CHUNK), 0)
    c = lax.broadcasted_iota(jnp.int32, (CHUNK, CHUNK), 1)
    tril_blk = ((c <= r) & (c // HG_BLOCK == r // HG_BLOCK)).astype(BF16)
    f_raws = [_dot(n, w_ref[:, HG_KDIM + cb * COL_BLOCK:HG_KDIM + (cb + 1) * COL_BLOCK])
              for cb in range(HG_KDIM // COL_BLOCK)]
    q_ref[...] = _dot(n, w_ref[:, 0:HG_KDIM]) * (HG_DK ** -0.5)
    v_ref[...] = _dot(n, w_ref[:, 2 * HG_KDIM:2 * HG_KDIM + HG_VDIM]).astype(BF16)
    for cb, f_raw in enumerate(f_raws):
        cols = slice(cb * COL_BLOCK, (cb + 1) * COL_BLOCK)
        f = lb[:, cols] + (1.0 - lb[:, cols]) * _sigmoid(f_raw)
        k_ref[:, cols] = 1.0 - f
        logf = jnp.log(f)
        for ci in range(N_CHUNKS):
            rows = slice(ci * CHUNK, (ci + 1) * CHUNK)
            gl_ref[rows, cols] = _dot_sel(tril_blk, logf[rows])

    t_i = lax.broadcasted_iota(jnp.int32, (CHUNK, 3 * CHUNK), 0)
    c_i = lax.broadcasted_iota(jnp.int32, (CHUNK, 3 * CHUNK), 1)
    s_i, grp = c_i % CHUNK, c_i // CHUNK
    bt, bs = t_i // HG_BLOCK, s_i // HG_BLOCK
    m_all = (((grp == 0) & (bt == bs) & (s_i <= t_i))
             | ((grp == 1) & (bt % 2 == 1) & (bs == bt - 1))
             | ((grp == 2) & (bt >= 2) & (bs < 2)))
    zero_blk = jnp.zeros((CHUNK, HG_DK), BF16)
    ones_blk = jnp.ones((HG_BLOCK, HG_KDIM), F32)
    gate_cols = HG_VDIM // N_CHUNKS

    atts = {}
    prepped = []
    for ci in range(N_CHUNKS):
        rows = slice(ci * CHUNK, (ci + 1) * CHUNK)
        q, k, gl, v = q_ref[rows, :], k_ref[rows, :], gl_ref[rows, :], v_ref[rows, :]
        tot = [gl[HG_BLOCK * i + HG_BLOCK - 1:HG_BLOCK * (i + 1), :] for i in range(4)]
        e_t = [jnp.exp(t) for t in tot]
        e01, e23 = e_t[0] * e_t[1], e_t[2] * e_t[3]
        dec = e01 * e23
        tot_rows = jnp.concatenate([_bcast_rows(t, HG_BLOCK) for t in tot], axis=0)
        qd = q * jnp.exp(gl)
        ki = k * jnp.exp(-gl)
        ke = k * jnp.exp(tot_rows - gl)
        qd32 = qd * jnp.concatenate(
            [ones_blk, _bcast_rows(e_t[0], HG_BLOCK), ones_blk, _bcast_rows(e_t[2], HG_BLOCK)], axis=0)
        ke32 = ke * jnp.concatenate(
            [_bcast_rows(e_t[1], HG_BLOCK), ones_blk, _bcast_rows(e_t[3], HG_BLOCK), ones_blk], axis=0)
        qd64 = qd32 * jnp.concatenate(
            [ones_blk, ones_blk, _bcast_rows(e01, HG_BLOCK), _bcast_rows(e01, HG_BLOCK)], axis=0)
        ke64 = ke32 * jnp.concatenate(
            [_bcast_rows(e23, HG_BLOCK), _bcast_rows(e23, HG_BLOCK), ones_blk, ones_blk], axis=0)
        qd, ki, ke, qd32, ke32, qd64, ke64 = (
            t.astype(BF16) for t in (qd, ki, ke, qd32, ke32, qd64, ke64))
        prepped.append((v, dec, qd64, ke64))
        g0 = 2 * HG_KDIM + HG_VDIM + ci * gate_cols
        gate_ref[:, ci * gate_cols:(ci + 1) * gate_cols] = _silu(_dot(n, w_ref[:, g0:g0 + gate_cols]))
        for h in range(HG_HEADS):
            kl = slice(h * HG_DK, (h + 1) * HG_DK)
            lhs = jnp.concatenate([qd[:, kl], qd32[:, kl]], axis=1)
            rhs_t = jnp.concatenate([jnp.concatenate([ki[:, kl], zero_blk], axis=1),
                                     jnp.concatenate([ke[:, kl], zero_blk], axis=1),
                                     jnp.concatenate([zero_blk, ke32[:, kl]], axis=1)], axis=0)
            atts[ci, h] = jnp.where(m_all, _dot_nt(lhs, rhs_t), 0.0).astype(BF16)
    gn = gn_ref[...]
    for ci in range(N_CHUNKS):
        rows = slice(ci * CHUNK, (ci + 1) * CHUNK)
        v, dec, qd64, ke64 = prepped[ci]
        for h in range(HG_HEADS):
            kl = slice(h * HG_DK, (h + 1) * HG_DK)
            vl = slice(h * HG_DV, (h + 1) * HG_DV)
            st = st_ref[h]
            v3 = jnp.concatenate([v[:, vl]] * 3, axis=0)
            oacc_ref[rows, vl] = _dot(atts[ci, h], v3) + _dot_nt(qd64[:, kl], st.astype(BF16))
            st_ref[h] = st * dec[:, kl] + _dot_tn(v[:, vl], ke64[:, kl])
        for h in range(HG_HEADS):
            vl = slice(h * HG_DV, (h + 1) * HG_DV)
            o_ref[rows, vl] = (_rms(oacc_ref[rows, vl], gn[:, vl]) * gate_ref[rows, vl]).astype(BF16)


def _hgrn(n, w, hg_lb, hg_norm, batch, seq):
    tiles = seq // SEQ_TILE
    seq_spec = pl.BlockSpec((SEQ_TILE, D_MODEL), lambda b, j: (b * tiles + j, 0))
    return pl.pallas_call(
        _hgrn_kernel,
        grid=(batch, tiles),
        in_specs=[seq_spec, _resident_cols(D_MODEL, 2 * HG_KDIM + 2 * HG_VDIM, 0), _resident(hg_lb.shape),
                  _resident((1, HG_VDIM))],
        out_specs=pl.BlockSpec((SEQ_TILE, HG_VDIM), lambda b, j: (b * tiles + j, 0)),
        out_shape=jax.ShapeDtypeStruct((batch * seq, HG_VDIM), BF16),
        scratch_shapes=[
            pltpu.VMEM((HG_HEADS, HG_DV, HG_DK), F32),
            pltpu.VMEM((SEQ_TILE, HG_KDIM), F32),
            pltpu.VMEM((SEQ_TILE, HG_KDIM), F32),
            pltpu.VMEM((SEQ_TILE, HG_KDIM), F32),
            pltpu.VMEM((SEQ_TILE, HG_VDIM), BF16),
            pltpu.VMEM((SEQ_TILE, HG_VDIM), F32),
            pltpu.VMEM((SEQ_TILE, HG_VDIM), F32),
        ],
        compiler_params=pltpu.CompilerParams(dimension_semantics=("arbitrary", "arbitrary"),
                                             vmem_limit_bytes=VMEM_LIMIT),
        name="hgrn2",
    )(n, w, hg_lb, hg_norm)


_PAIRS = M_HEADS // 2
_PAIR_W = 2 * M_HEADDIM


def _ssd_kernel(n_ref, wz_ref, wxbc_ref, wdt_ref, cw_ref, cb_ref, dtb_ref, aneg_ref,
                dsk_ref, gn_ref, y_ref,
                st_ref, hist_ref, edge_ref, xbc_ref, zs_ref, yacc_ref):
    @pl.when(pl.program_id(1) == 0)
    def _():
        st_ref[...] = jnp.zeros_like(st_ref)
        hist_ref[...] = jnp.zeros_like(hist_ref)

    n = n_ref[...].astype(BF16)
    r = lax.broadcasted_iota(jnp.int32, (CHUNK, CHUNK), 0)
    c = lax.broadcasted_iota(jnp.int32, (CHUNK, CHUNK), 1)
    tril_chunk = (c <= r).astype(BF16)
    e_r = lax.broadcasted_iota(jnp.int32, (_PAIR_W, COL_BLOCK), 0)
    e_c = lax.broadcasted_iota(jnp.int32, (_PAIR_W, COL_BLOCK), 1)
    lane_head = jnp.where(e_r < _PAIRS, 2 * e_r, jnp.where(e_r < M_HEADS, 2 * (e_r - _PAIRS) + 1, -1))

    dt_c = _softplus(_dot(n, wdt_ref[...]) + dtb_ref[...])
    a_c = dt_c * aneg_ref[...]
    acs_c = jnp.concatenate([_dot_sel(tril_chunk, a_c[ci * CHUNK:(ci + 1) * CHUNK])
                             for ci in range(N_CHUNKS)], axis=0)
    dt_parts = jnp.concatenate(_split_bf16(dt_c, 2), axis=1)
    acs_parts = jnp.concatenate(_split_bf16(acs_c, 3), axis=1)
    acs_t = acs_c.T[0:M_HEADS]

    first_expand = (M_CONV_DIM - M_DINNER) // COL_BLOCK
    dt_blocks, acs_blocks = [], []
    for cb_i in range(M_CONV_DIM // COL_BLOCK):
        cols = slice(cb_i * COL_BLOCK, (cb_i + 1) * COL_BLOCK)
        x = _dot(n, wxbc_ref[:, cols])
        cw = cw_ref[:, cols]
        cb = cb_ref[:, cols]
        x1 = pltpu.roll(x, 1, 0)
        near = x * cw[3:4] + x1 * cw[2:3]
        far = x * cw[1:2] + x1 * cw[0:1]
        xbc_ref[:, cols] = _silu(near + pltpu.roll(far, 2, 0) + cb)
        edge_ref[0:8, cols] = hist_ref[:, cols]
        edge_ref[8:16, cols] = x[0:8]
        top = _bcast_rows(cb, 8)
        for kk in range(M_CONV):
            top = top + edge_ref[8 - (M_CONV - 1) + kk:16 - (M_CONV - 1) + kk, cols] * cw[kk:kk + 1]
        xbc_ref[0:8, cols] = _silu(top)
        hist_ref[:, cols] = x[SEQ_TILE - 8:SEQ_TILE]
        if cb_i >= first_expand:
            expand = ((e_c + (cb_i - first_expand) * COL_BLOCK) // M_HEADDIM == lane_head).astype(BF16)
            dt_blocks.append(_dot(dt_parts, jnp.concatenate([expand] * 2, axis=0)))
            acs_blocks.append(_dot(acs_parts, jnp.concatenate([expand] * 3, axis=0)))
    dt_e = jnp.concatenate(dt_blocks, axis=1)
    acs_e = jnp.concatenate(acs_blocks, axis=1)

    lane_left = lax.broadcasted_iota(jnp.int32, (_PAIRS, _PAIR_W), 1) < CHUNK
    acs_pair = []
    for vc in range(SEQ_TILE // _PAIR_W):
        ev = acs_t[0:_PAIRS, vc * _PAIR_W:(vc + 1) * _PAIR_W]
        od = acs_t[_PAIRS:2 * _PAIRS, vc * _PAIR_W:(vc + 1) * _PAIR_W]
        acs_pair.append(jnp.where(lane_left, ev, pltpu.roll(od, CHUNK, 1)))
        acs_pair.append(jnp.where(lane_left, pltpu.roll(ev, CHUNK, 1), od))

    xdt = xbc_ref[:, 0:M_DINNER] * dt_e
    w_start = jnp.exp(acs_e)

    t_i = lax.broadcasted_iota(jnp.int32, (CHUNK, _PAIR_W), 0)
    l_i = lax.broadcasted_iota(jnp.int32, (CHUNK, _PAIR_W), 1)
    causal2 = (l_i % CHUNK) <= t_i
    left = l_i < M_HEADDIM

    row_chunk = lax.broadcasted_iota(jnp.int32, (SEQ_TILE, M_STATE), 0) // CHUNK
    acs_last = [acs_e[(ci + 1) * CHUNK - 1:(ci + 1) * CHUNK] for ci in range(N_CHUNKS)]
    acs_last_rows = jnp.concatenate([_bcast_rows(a, CHUNK) for a in acs_last], axis=0)
    xw = (xdt * jnp.exp(acs_last_rows - acs_e)).astype(BF16)
    xdt_b = xdt.astype(BF16)

    def by_chunk(v):
        zero = jnp.zeros_like(v)
        return jnp.concatenate([jnp.where(row_chunk == ci, v, zero) for ci in range(N_CHUNKS)], axis=1)

    for g in range(M_GROUPS):
        gl = slice(g * M_GROUP_W, (g + 1) * M_GROUP_W)
        b_g = xbc_ref[:, M_DINNER + g * M_STATE:M_DINNER + (g + 1) * M_STATE].astype(BF16)
        c_g = xbc_ref[:, M_DINNER + (M_GROUPS + g) * M_STATE:
                      M_DINNER + (M_GROUPS + g + 1) * M_STATE].astype(BF16)
        zs_ref[:, gl] = _silu(_dot(n, wz_ref[:, gl]))
        b_dup = jnp.concatenate([b_g[ci * CHUNK:(ci + 1) * CHUNK] for ci in range(N_CHUNKS) for _ in (0, 1)],
                                axis=0)
        cb_all = _dot_nt(c_g, b_dup)
        for ci in range(N_CHUNKS):
            rows = slice(ci * CHUNK, (ci + 1) * CHUNK)
            cb2 = cb_all[rows, ci * _PAIR_W:(ci + 1) * _PAIR_W]
            for pp in range(M_GROUP_W // _PAIR_W):
                p = g * (M_GROUP_W // _PAIR_W) + pp
                pl_ = slice(p * _PAIR_W, (p + 1) * _PAIR_W)
                seg = acs_e[rows, pl_] - acs_pair[ci][p:p + 1, :]
                decay = jnp.exp(jnp.where(causal2, seg, -jnp.inf))
                m = (cb2 * decay).astype(BF16)
                xp = xdt_b[rows, pl_]
                zero = jnp.zeros_like(xp)
                rhs = jnp.concatenate([jnp.where(left, xp, zero), jnp.where(left, zero, xp)], axis=0)
                yacc_ref[rows, pl_] = _dot(m, rhs)
        incr = _dot_tn(by_chunk(b_g), xw[:, gl])
        st = st_ref[g]
        starts = []
        for ci in range(N_CHUNKS):
            starts.append(st.astype(BF16))
            st = st * jnp.exp(acs_last[ci][:, gl]) + incr[ci * M_STATE:(ci + 1) * M_STATE]
        st_ref[g] = st
        y_inter = _dot(by_chunk(c_g), jnp.concatenate(starts, axis=0))
        y = (yacc_ref[:, gl] + y_inter * w_start[:, gl] + dsk_ref[:, gl] * xbc_ref[:, gl]) * zs_ref[:, gl]
        y_ref[:, gl] = _rms(y, gn_ref[:, gl]).astype(BF16)


def _ssd(n, w_all, wdt, cw, cb, dtb, aneg, dsk, gn, batch, seq):
    tiles = seq // SEQ_TILE
    seq_spec = pl.BlockSpec((SEQ_TILE, D_MODEL), lambda b, j: (b * tiles + j, 0))
    params = (wdt, cw, cb, dtb, aneg, dsk, gn)
    z_off, xbc_off = 2 * HG_KDIM + 2 * HG_VDIM, 2 * HG_KDIM + 2 * HG_VDIM + M_DINNER
    assert z_off % M_DINNER == 0 and xbc_off % M_CONV_DIM == 0
    return pl.pallas_call(
        _ssd_kernel,
        grid=(batch, tiles),
        in_specs=[seq_spec, _resident_cols(D_MODEL, M_DINNER, z_off // M_DINNER),
                  _resident_cols(D_MODEL, M_CONV_DIM, xbc_off // M_CONV_DIM)]
                 + [_resident(a.shape) for a in params],
        out_specs=pl.BlockSpec((SEQ_TILE, M_DINNER), lambda b, j: (b * tiles + j, 0)),
        out_shape=jax.ShapeDtypeStruct((batch * seq, M_DINNER), BF16),
        scratch_shapes=[
            pltpu.VMEM((M_GROUPS, M_STATE, M_GROUP_W), F32),
            pltpu.VMEM((8, M_CONV_DIM), F32),
            pltpu.VMEM((16, M_CONV_DIM), F32),
            pltpu.VMEM((SEQ_TILE, M_CONV_DIM), F32),
            pltpu.VMEM((SEQ_TILE, M_DINNER), F32),
            pltpu.VMEM((SEQ_TILE, M_DINNER), F32),
        ],
        compiler_params=pltpu.CompilerParams(dimension_semantics=("arbitrary", "arbitrary"),
                                             vmem_limit_bytes=VMEM_LIMIT),
        name="ssd",
    )(n, w_all, w_all, *params)


def _merge_kernel(h_ref, n_ref, ohg_ref, y_ref, wbr_ref, whg_ref, wssm_ref, wout_ref, o_ref):
    br = _sigmoid(_dot(n_ref[...].astype(BF16), wbr_ref[...]))
    mixed = (br[:, 0:D_MODEL] * _dot(ohg_ref[...], whg_ref[...])
             + br[:, D_MODEL:2 * D_MODEL] * _dot(y_ref[...], wssm_ref[...]))
    o_ref[...] = h_ref[...] + _dot(mixed.astype(BF16), wout_ref[...])


def _merge(h, n, ohg, y, wbr, whg, wssm, wout):
    t = h.shape[0]
    return pl.pallas_call(
        _merge_kernel,
        grid=(t // ROW_TILE,),
        in_specs=[_row_spec(D_MODEL), _row_spec(D_MODEL), _row_spec(HG_VDIM), _row_spec(M_DINNER),
                  _resident(wbr.shape), _resident(whg.shape), _resident(wssm.shape),
                  _resident(wout.shape)],
        out_specs=_row_spec(D_MODEL),
        out_shape=jax.ShapeDtypeStruct((t, D_MODEL), F32),
        compiler_params=pltpu.CompilerParams(dimension_semantics=("arbitrary",),
                                             vmem_limit_bytes=VMEM_LIMIT),
        name="merge",
    )(h, n, ohg, y, wbr, whg, wssm, wout)


def kernel(x, p, ffn1_norm, ffn1_w13, ffn1_w2, mix_norm, w_in, conv_w, conv_b, dt_bias, a_log,
           d_skip, ssm_norm, hg_lb, hg_norm, w_hg_out, w_ssm_out, w_out, ffn2_norm, ffn2_w13,
           ffn2_w2, ple_norm, w_ple_gate, w_ple_proj, final_norm):
    batch, seq, _ = x.shape
    assert ffn1_w13.shape[0] == 1 and hg_lb.shape[0] == 2, "single-layer block"
    assert seq % SEQ_TILE == 0 and (batch * seq) % ROW_TILE == 0
    t = batch * seq
    row = lambda v: v.reshape(1, -1).astype(F32)
    expand = lambda v: jnp.repeat(v, M_HEADDIM, axis=-1)
    pad_heads = lambda v: jnp.pad(v, [(0, 0)] * (v.ndim - 1) + [(0, _PAIR_W - M_HEADS)])
    even_odd = lambda v: jnp.concatenate([v[..., 0::2], v[..., 1::2]], axis=-1)

    sizes = (HG_KDIM, HG_KDIM, HG_VDIM, HG_VDIM, M_DINNER, M_CONV_DIM, M_HEADS, 2 * D_MODEL)
    offs = [0]
    for s in sizes:
        offs.append(offs[-1] + s)
    w = w_in[0].astype(BF16)
    w_dt = w[:, offs[6]:offs[7]]
    w_br = w[:, offs[7]:offs[8]]
    a_neg = -jnp.exp(a_log[0].astype(F32))
    dtb = dt_bias[0].astype(F32)

    x2 = x.reshape(t, D_MODEL)
    h1, n = _ffn1(x2, row(ffn1_norm[0]), ffn1_w13[0].astype(BF16), ffn1_w2[0].astype(BF16),
                  row(mix_norm[0]))
    o_hg = _hgrn(n, w, hg_lb.astype(F32), row(hg_norm[0]), batch, seq)
    y = _ssd(n, w, pad_heads(even_odd(w_dt)),
             conv_w[0].astype(F32), row(conv_b[0]), row(pad_heads(even_odd(dtb))),
             row(pad_heads(even_odd(a_neg))),
             row(expand(d_skip[0].astype(F32))), row(ssm_norm[0]), batch, seq)
    h2 = _merge(h1, n, o_hg, y, w_br, w_hg_out[0].astype(BF16), w_ssm_out[0].astype(BF16),
                w_out[0].astype(BF16))
    out = _ffn2(h2, row(ffn2_norm[0]), ffn2_w13[0].astype(BF16), ffn2_w2[0].astype(BF16),
                p[0].reshape(t, PLE_DIM), row(ple_norm[0]), w_ple_gate[0].astype(BF16),
                w_ple_proj[0].astype(BF16), row(final_norm))
    return out.reshape(batch, seq, D_MODEL)
```

```python
import jax
import jax.numpy as jnp
from jax import lax
from jax.experimental import pallas as pl
from jax.experimental.pallas import tpu as pltpu

F32 = jnp.float32
BF16 = jnp.bfloat16

D_MODEL = 1024
D_FF = 2816
PLE_DIM = 256
EPS = 1e-6
HG_HEADS = 8
HG_DK = 128
HG_DV = 128
HG_KDIM = HG_HEADS * HG_DK
HG_VDIM = HG_HEADS * HG_DV
HG_BLOCK = 16
M_DINNER = 2048
M_HEADDIM = 64
M_HEADS = 32
M_STATE = 128
M_GROUPS = 4
M_CONV = 4
M_GROUP_W = M_DINNER // M_GROUPS
M_CONV_DIM = M_DINNER + 2 * M_GROUPS * M_STATE
CHUNK = 64

ROW_TILE = 512
SEQ_TILE = 256
N_CHUNKS = SEQ_TILE // CHUNK
COL_BLOCK = 512
VMEM_LIMIT = 56 * 1024 * 1024


def _dot(a, b):
    return jnp.dot(a, b, preferred_element_type=F32)


def _dot_nt(a, b):
    return lax.dot_general(a, b, (((1,), (1,)), ((), ())), preferred_element_type=F32)


def _dot_tn(a, b):
    return lax.dot_general(a, b, (((0,), (0,)), ((), ())), preferred_element_type=F32)


def _split_bf16(x, parts):
    out, rem = [], x
    for _ in range(parts):
        p = rem.astype(BF16)
        out.append(p)
        rem = rem - p.astype(F32)
    return out


def _dot_sel(sel, x, parts=3):
    return _dot(jnp.concatenate([sel] * parts, axis=1), jnp.concatenate(_split_bf16(x, parts), axis=0))


def _dot_sel_r(x, sel, parts=3):
    return _dot(jnp.concatenate(_split_bf16(x, parts), axis=1), jnp.concatenate([sel] * parts, axis=0))


def _rms(x, g):
    ms = jnp.mean(x * x, axis=-1, keepdims=True)
    return x * lax.rsqrt(ms + EPS) * g


def _sigmoid(x):
    return 0.5 * jnp.tanh(0.5 * x) + 0.5


def _silu(x):
    h = 0.5 * x
    return h + h * jnp.tanh(h)


def _softplus(x):
    return jnp.maximum(x, 0.0) + jnp.log(1.0 + jnp.exp(-jnp.abs(x)))


def _bcast_rows(row, rows):
    return jnp.broadcast_to(row, (rows, row.shape[-1]))


def _resident(shape):
    nd = len(shape)
    return pl.BlockSpec(shape, lambda *_: (0,) * nd, pipeline_mode=pl.Buffered(1))


def _resident_cols(rows, width, block):
    return pl.BlockSpec((rows, width), lambda *_: (0, block), pipeline_mode=pl.Buffered(1))


def _swiglu_residual(x, g_ref, w13_ref, w2_ref):
    n = _rms(x, g_ref[...]).astype(BF16)
    gate = _dot(n, w13_ref[:, 0:D_FF])
    up = _dot(n, w13_ref[:, D_FF:2 * D_FF])
    act = (_silu(gate) * up).astype(BF16)
    return x + 0.5 * _dot(act, w2_ref[...])


def _ffn1_kernel(x_ref, g_ref, w13_ref, w2_ref, mixg_ref, h_ref, n_ref):
    h = _swiglu_residual(x_ref[...], g_ref, w13_ref, w2_ref)
    h_ref[...] = h
    n_ref[...] = _rms(h, mixg_ref[...])


def _ffn2_kernel(x_ref, g_ref, w13_ref, w2_ref, p_ref, pleg_ref, wpg_ref, wpp_ref, fing_ref, o_ref):
    h = _swiglu_residual(x_ref[...], g_ref, w13_ref, w2_ref)
    gate = _sigmoid(_dot(_rms(h, pleg_ref[...]).astype(BF16), wpg_ref[...]))
    emb = _dot(p_ref[...].astype(BF16), wpp_ref[...])
    h = h + gate * emb
    o_ref[...] = _rms(h, fing_ref[...])


def _row_spec(width):
    return pl.BlockSpec((ROW_TILE, width), lambda i: (i, 0))


def _ffn1(x, g, w13, w2, mixg):
    t = x.shape[0]
    return pl.pallas_call(
        _ffn1_kernel,
        grid=(t // ROW_TILE,),
        in_specs=[_row_spec(D_MODEL), _resident((1, D_MODEL)), _resident(w13.shape),
                  _resident(w2.shape), _resident((1, D_MODEL))],
        out_specs=[_row_spec(D_MODEL), _row_spec(D_MODEL)],
        out_shape=[jax.ShapeDtypeStruct((t, D_MODEL), F32),
                   jax.ShapeDtypeStruct((t, D_MODEL), F32)],
        compiler_params=pltpu.CompilerParams(dimension_semantics=("arbitrary",),
                                             vmem_limit_bytes=VMEM_LIMIT),
        name="ffn1",
    )(x, g, w13, w2, mixg)


def _ffn2(x, g, w13, w2, p, pleg, wpg, wpp, fing):
    t = x.shape[0]
    return pl.pallas_call(
        _ffn2_kernel,
        grid=(t // ROW_TILE,),
        in_specs=[_row_spec(D_MODEL), _resident((1, D_MODEL)), _resident(w13.shape),
                  _resident(w2.shape), _row_spec(PLE_DIM), _resident((1, D_MODEL)),
                  _resident(wpg.shape), _resident(wpp.shape), _resident((1, D_MODEL))],
        out_specs=_row_spec(D_MODEL),
        out_shape=jax.ShapeDtypeStruct((t, D_MODEL), F32),
        compiler_params=pltpu.CompilerParams(dimension_semantics=("arbitrary",),
                                             vmem_limit_bytes=VMEM_LIMIT),
        name="ffn2",
    )(x, g, w13, w2, p, pleg, wpg, wpp, fing)


def _hgrn_kernel(n_ref, w_ref, lb_ref, gn_ref, o_ref,
                 st_ref, q_ref, k_ref, gl_ref, v_ref, gate_ref, oacc_ref):
    @pl.when(pl.program_id(1) == 0)
    def _():
        st_ref[...] = jnp.zeros_like(st_ref)

    n = n_ref[...].astype(BF16)
    a = lb_ref[...]
    e = jnp.exp(a - jnp.max(a, axis=0, keepdims=True))
    lb = e[0:1] / jnp.sum(e, axis=0, keepdims=True)

    r = lax.broadcasted_iota(jnp.int32, (CHUNK, CHUNK), 0)
    c = lax.broadcasted_iota(jnp.int32, (CHUNK, CHUNK), 1)
    tril_blk = ((c <= r) & (c // HG_BLOCK == r // HG_BLOCK)).astype(BF16)
    f_raws = [_dot(n, w_ref[:, HG_KDIM + cb * COL_BLOCK:HG_KDIM + (cb + 1) * COL_BLOCK])
              for cb in range(HG_KDIM // COL_BLOCK)]
    q_ref[...] = _dot(n, w_ref[:, 0:HG_KDIM]) * (HG_DK ** -0.5)
    v_ref[...] = _dot(n, w_ref[:, 2 * HG_KDIM:2 * HG_KDIM + HG_VDIM]).astype(BF16)
    for cb, f_raw in enumerate(f_raws):
        cols = slice(cb * COL_BLOCK, (cb + 1) * COL_BLOCK)
        f = lb[:, cols] + (1.0 - lb[:, cols]) * _sigmoid(f_raw)
        k_ref[:, cols] = 1.0 - f
        logf = jnp.log(f)
        for ci in range(N_CHUNKS):
            rows = slice(ci * CHUNK, (ci + 1) * CHUNK)
            gl_ref[rows, cols] = _dot_sel(tril_blk, logf[rows])

    t_i = lax.broadcasted_iota(jnp.int32, (CHUNK, 4 * CHUNK), 0)
    c_i = lax.broadcasted_iota(jnp.int32, (CHUNK, 4 * CHUNK), 1)
    s_i, grp = c_i % CHUNK, c_i // CHUNK
    bt, bs = t_i // HG_BLOCK, s_i // HG_BLOCK
    m_all = (((grp == 0) & (bt == bs) & (s_i <= t_i))
             | ((grp == 1) & (bt % 2 == 1) & (bs == bt - 1))
             | ((grp == 2) & (bt >= 2) & (bs < 2)))
    zero_blk = jnp.zeros((CHUNK, HG_DK), BF16)
    ones_blk = jnp.ones((HG_BLOCK, HG_KDIM), F32)
    gate_cols = HG_VDIM // N_CHUNKS

    atts = {}
    prepped = []
    for ci in range(N_CHUNKS):
        rows = slice(ci * CHUNK, (ci + 1) * CHUNK)
        q, k, gl, v = q_ref[rows, :], k_ref[rows, :], gl_ref[rows, :], v_ref[rows, :]
        tot = [gl[HG_BLOCK * i + HG_BLOCK - 1:HG_BLOCK * (i + 1), :] for i in range(4)]
        e_t = [jnp.exp(t) for t in tot]
        e01, e23 = e_t[0] * e_t[1], e_t[2] * e_t[3]
        dec = e01 * e23
        tot_rows = jnp.concatenate([_bcast_rows(t, HG_BLOCK) for t in tot], axis=0)
        qd = q * jnp.exp(gl)
        ki = k * jnp.exp(-gl)
        ke = k * jnp.exp(tot_rows - gl)
        qd32 = qd * jnp.concatenate(
            [ones_blk, _bcast_rows(e_t[0], HG_BLOCK), ones_blk, _bcast_rows(e_t[2], HG_BLOCK)], axis=0)
        ke32 = ke * jnp.concatenate(
            [_bcast_rows(e_t[1], HG_BLOCK), ones_blk, _bcast_rows(e_t[3], HG_BLOCK), ones_blk], axis=0)
        qd64 = qd32 * jnp.concatenate(
            [ones_blk, ones_blk, _bcast_rows(e01, HG_BLOCK), _bcast_rows(e01, HG_BLOCK)], axis=0)
        ke64 = ke32 * jnp.concatenate(
            [_bcast_rows(e23, HG_BLOCK), _bcast_rows(e23, HG_BLOCK), ones_blk, ones_blk], axis=0)
        qd, ki, ke, qd32, ke32, qd64, ke64 = (
            t.astype(BF16) for t in (qd, ki, ke, qd32, ke32, qd64, ke64))
        prepped.append((v, dec, qd64, ke64))
        g0 = 2 * HG_KDIM + HG_VDIM + ci * gate_cols
        gate_ref[:, ci * gate_cols:(ci + 1) * gate_cols] = _silu(_dot(n, w_ref[:, g0:g0 + gate_cols]))
        for h in range(HG_HEADS):
            kl = slice(h * HG_DK, (h + 1) * HG_DK)
            lhs = jnp.concatenate([qd[:, kl], qd32[:, kl]], axis=1)
            rhs_t = jnp.concatenate([jnp.concatenate([ki[:, kl], zero_blk], axis=1),
                                     jnp.concatenate([ke[:, kl], zero_blk], axis=1),
                                     jnp.concatenate([zero_blk, ke32[:, kl]], axis=1),
                                     jnp.concatenate([zero_blk, zero_blk], axis=1)], axis=0)
            atts[ci, h] = jnp.where(m_all, _dot_nt(lhs, rhs_t), 0.0).astype(BF16)
    gn = gn_ref[...]
    first_head = lax.broadcasted_iota(jnp.int32, (CHUNK, 2 * HG_DV), 1) < HG_DV
    for ci in range(N_CHUNKS):
        rows = slice(ci * CHUNK, (ci + 1) * CHUNK)
        v, dec, qd64, ke64 = prepped[ci]
        for h in range(0, HG_HEADS, 2):
            kl2 = slice(h * HG_DK, (h + 2) * HG_DK)
            vl2 = slice(h * HG_DV, (h + 2) * HG_DV)
            st_a, st_b = st_ref[h], st_ref[h + 1]
            v2 = v[:, vl2]
            zero_v = jnp.zeros_like(v2)
            v_a = jnp.where(first_head, v2, zero_v)
            v_b = jnp.where(first_head, zero_v, v2)
            v_bd = jnp.concatenate([v_a, v_a, v_a, zero_v, v_b, v_b, v_b, zero_v], axis=0)
            zero_s = jnp.zeros((HG_DV, HG_DK), BF16)
            st_bd = jnp.concatenate([jnp.concatenate([st_a.astype(BF16), zero_s], axis=1),
                                     jnp.concatenate([zero_s, st_b.astype(BF16)], axis=1)], axis=0)
            att2 = jnp.concatenate([atts[ci, h], atts[ci, h + 1]], axis=1)
            oacc_ref[rows, vl2] = _dot(att2, v_bd) + _dot_nt(qd64[:, kl2], st_bd)
            incr = _dot_tn(v2, ke64[:, kl2])
            st_ref[h] = st_a * dec[:, h * HG_DK:(h + 1) * HG_DK] + incr[0:HG_DV, 0:HG_DK]
            st_ref[h + 1] = (st_b * dec[:, (h + 1) * HG_DK:(h + 2) * HG_DK]
                             + incr[HG_DV:2 * HG_DV, HG_DK:2 * HG_DK])
        for h in range(HG_HEADS):
            vl = slice(h * HG_DV, (h + 1) * HG_DV)
            o_ref[rows, vl] = (_rms(oacc_ref[rows, vl], gn[:, vl]) * gate_ref[rows, vl]).astype(BF16)


def _hgrn(n, w, hg_lb, hg_norm, batch, seq):
    tiles = seq // SEQ_TILE
    seq_spec = pl.BlockSpec((SEQ_TILE, D_MODEL), lambda b, j: (b * tiles + j, 0))
    return pl.pallas_call(
        _hgrn_kernel,
        grid=(batch, tiles),
        in_specs=[seq_spec, _resident_cols(D_MODEL, 2 * HG_KDIM + 2 * HG_VDIM, 0), _resident(hg_lb.shape),
                  _resident((1, HG_VDIM))],
        out_specs=pl.BlockSpec((SEQ_TILE, HG_VDIM), lambda b, j: (b * tiles + j, 0)),
        out_shape=jax.ShapeDtypeStruct((batch * seq, HG_VDIM), BF16),
        scratch_shapes=[
            pltpu.VMEM((HG_HEADS, HG_DV, HG_DK), F32),
            pltpu.VMEM((SEQ_TILE, HG_KDIM), F32),
            pltpu.VMEM((SEQ_TILE, HG_KDIM), F32),
            pltpu.VMEM((SEQ_TILE, HG_KDIM), F32),
            pltpu.VMEM((SEQ_TILE, HG_VDIM), BF16),
            pltpu.VMEM((SEQ_TILE, HG_VDIM), F32),
            pltpu.VMEM((SEQ_TILE, HG_VDIM), F32),
        ],
        compiler_params=pltpu.CompilerParams(dimension_semantics=("arbitrary", "arbitrary"),
                                             vmem_limit_bytes=VMEM_LIMIT),
        name="hgrn2",
    )(n, w, hg_lb, hg_norm)


_PAIRS = M_HEADS // 2
_PAIR_W = 2 * M_HEADDIM


def _ssd_kernel(n_ref, wz_ref, wxbc_ref, wdt_ref, wdtt_ref, cw_ref, cb_ref, dtb_ref, aneg_ref,
                dtbt_ref, anegt_ref, dsk_ref, gn_ref, y_ref,
                st_ref, hist_ref, edge_ref, xbc_ref, zs_ref, yacc_ref):
    @pl.when(pl.program_id(1) == 0)
    def _():
        st_ref[...] = jnp.zeros_like(st_ref)
        hist_ref[...] = jnp.zeros_like(hist_ref)

    n = n_ref[...].astype(BF16)
    r = lax.broadcasted_iota(jnp.int32, (SEQ_TILE, SEQ_TILE), 0)
    c = lax.broadcasted_iota(jnp.int32, (SEQ_TILE, SEQ_TILE), 1)
    same_chunk = c // CHUNK == r // CHUNK
    tril_chunk = ((c <= r) & same_chunk).astype(BF16)
    triu_chunk = ((r <= c) & same_chunk).astype(BF16)
    e_r = lax.broadcasted_iota(jnp.int32, (_PAIR_W, COL_BLOCK), 0)
    e_c = lax.broadcasted_iota(jnp.int32, (_PAIR_W, COL_BLOCK), 1)

    dt_c = _softplus(_dot(n, wdt_ref[...]) + dtb_ref[...])
    a_c = dt_c * aneg_ref[...]
    acs_c = jnp.concatenate([_dot_sel(tril_chunk[0:CHUNK, 0:CHUNK], a_c[ci * CHUNK:(ci + 1) * CHUNK])
                             for ci in range(N_CHUNKS)], axis=0)
    dt_parts = jnp.concatenate(_split_bf16(dt_c, 2), axis=1)
    acs_parts = jnp.concatenate(_split_bf16(acs_c, 3), axis=1)
    a_t = _softplus(_dot_nt(wdtt_ref[...], n) + dtbt_ref[...]) * anegt_ref[...]
    acs_t = _dot_sel_r(a_t, triu_chunk, 3)

    first_expand = (M_CONV_DIM - M_DINNER) // COL_BLOCK
    dt_blocks, acs_blocks = [], []
    for cb_i in range(M_CONV_DIM // COL_BLOCK):
        cols = slice(cb_i * COL_BLOCK, (cb_i + 1) * COL_BLOCK)
        x = _dot(n, wxbc_ref[:, cols])
        cw = cw_ref[:, cols]
        cb = cb_ref[:, cols]
        x1 = pltpu.roll(x, 1, 0)
        near = x * cw[3:4] + x1 * cw[2:3]
        far = x * cw[1:2] + x1 * cw[0:1]
        xbc_ref[:, cols] = _silu(near + pltpu.roll(far, 2, 0) + cb)
        edge_ref[0:8, cols] = hist_ref[:, cols]
        edge_ref[8:16, cols] = x[0:8]
        top = _bcast_rows(cb, 8)
        for kk in range(M_CONV):
            top = top + edge_ref[8 - (M_CONV - 1) + kk:16 - (M_CONV - 1) + kk, cols] * cw[kk:kk + 1]
        xbc_ref[0:8, cols] = _silu(top)
        hist_ref[:, cols] = x[SEQ_TILE - 8:SEQ_TILE]
        if cb_i >= first_expand:
            expand = ((e_c + (cb_i - first_expand) * COL_BLOCK) // M_HEADDIM == e_r).astype(BF16)
            dt_blocks.append(_dot(dt_parts, jnp.concatenate([expand] * 2, axis=0)))
            acs_blocks.append(_dot(acs_parts, jnp.concatenate([expand] * 3, axis=0)))
    dt_e = jnp.concatenate(dt_blocks, axis=1)
    acs_e = jnp.concatenate(acs_blocks, axis=1)

    lane_left = lax.broadcasted_iota(jnp.int32, (_PAIRS, _PAIR_W), 1) < CHUNK
    acs_pair = []
    for vc in range(SEQ_TILE // _PAIR_W):
        ev = acs_t[0:_PAIRS, vc * _PAIR_W:(vc + 1) * _PAIR_W]
        od = acs_t[_PAIRS:2 * _PAIRS, vc * _PAIR_W:(vc + 1) * _PAIR_W]
        acs_pair.append(jnp.where(lane_left, ev, pltpu.roll(od, CHUNK, 1)))
        acs_pair.append(jnp.where(lane_left, pltpu.roll(ev, CHUNK, 1), od))

    xdt = xbc_ref[:, 0:M_DINNER] * dt_e
    w_start = jnp.exp(acs_e)

    t_i = lax.broadcasted_iota(jnp.int32, (CHUNK, _PAIR_W), 0)
    l_i = lax.broadcasted_iota(jnp.int32, (CHUNK, _PAIR_W), 1)
    causal2 = (l_i % CHUNK) <= t_i
    left = l_i < M_HEADDIM

    row_chunk = lax.broadcasted_iota(jnp.int32, (SEQ_TILE, M_STATE), 0) // CHUNK
    acs_last = [acs_e[(ci + 1) * CHUNK - 1:(ci + 1) * CHUNK] for ci in range(N_CHUNKS)]
    acs_last_rows = jnp.concatenate([_bcast_rows(a, CHUNK) for a in acs_last], axis=0)
    xw = (xdt * jnp.exp(acs_last_rows - acs_e)).astype(BF16)
    xdt_b = xdt.astype(BF16)

    def by_chunk(v):
        zero = jnp.zeros_like(v)
        return jnp.concatenate([jnp.where(row_chunk == ci, v, zero) for ci in range(N_CHUNKS)], axis=1)

    for g in range(M_GROUPS):
        gl = slice(g * M_GROUP_W, (g + 1) * M_GROUP_W)
        b_g = xbc_ref[:, M_DINNER + g * M_STATE:M_DINNER + (g + 1) * M_STATE].astype(BF16)
        c_g = xbc_ref[:, M_DINNER + (M_GROUPS + g) * M_STATE:
                      M_DINNER + (M_GROUPS + g + 1) * M_STATE].astype(BF16)
        zs_ref[:, gl] = _silu(_dot(n, wz_ref[:, gl]))
        b_dup = jnp.concatenate([b_g[ci * CHUNK:(ci + 1) * CHUNK] for ci in range(N_CHUNKS) for _ in (0, 1)],
                                axis=0)
        cb_all = _dot_nt(c_g, b_dup)
        for ci in range(N_CHUNKS):
            rows = slice(ci * CHUNK, (ci + 1) * CHUNK)
            cb2 = cb_all[rows, ci * _PAIR_W:(ci + 1) * _PAIR_W]
            for qq in range(M_GROUP_W // (2 * _PAIR_W)):
                ms, rhss = [], []
                for pp in (2 * qq, 2 * qq + 1):
                    p = g * (M_GROUP_W // _PAIR_W) + pp
                    pl_ = slice(p * _PAIR_W, (p + 1) * _PAIR_W)
                    seg = acs_e[rows, pl_] - acs_pair[ci][p:p + 1, :]
                    decay = jnp.exp(jnp.where(causal2, seg, -jnp.inf))
                    ms.append((cb2 * decay).astype(BF16))
                    xp = xdt_b[rows, pl_]
                    zero = jnp.zeros_like(xp)
                    rhss.append(jnp.concatenate([jnp.where(left, xp, zero), jnp.where(left, zero, xp)], axis=0))
                zblk = jnp.zeros_like(rhss[0])
                rhs = jnp.concatenate([jnp.concatenate([rhss[0], zblk], axis=1),
                                       jnp.concatenate([zblk, rhss[1]], axis=1)], axis=0)
                q0 = (g * (M_GROUP_W // _PAIR_W) + 2 * qq) * _PAIR_W
                yacc_ref[rows, q0:q0 + 2 * _PAIR_W] = _dot(jnp.concatenate(ms, axis=1), rhs)
        incr = _dot_tn(by_chunk(b_g), xw[:, gl])
        st = st_ref[g]
        starts = []
        for ci in range(N_CHUNKS):
            starts.append(st.astype(BF16))
            st = st * jnp.exp(acs_last[ci][:, gl]) + incr[ci * M_STATE:(ci + 1) * M_STATE]
        st_ref[g] = st
        y_inter = _dot(by_chunk(c_g), jnp.concatenate(starts, axis=0))
        y = (yacc_ref[:, gl] + y_inter * w_start[:, gl] + dsk_ref[:, gl] * xbc_ref[:, gl]) * zs_ref[:, gl]
        y_ref[:, gl] = _rms(y, gn_ref[:, gl]).astype(BF16)


def _ssd(n, w_all, wdt, wdtt, cw, cb, dtb, aneg, dtbt, anegt, dsk, gn, batch, seq):
    tiles = seq // SEQ_TILE
    seq_spec = pl.BlockSpec((SEQ_TILE, D_MODEL), lambda b, j: (b * tiles + j, 0))
    params = (wdt, wdtt, cw, cb, dtb, aneg, dtbt, anegt, dsk, gn)
    z_off, xbc_off = 2 * HG_KDIM + 2 * HG_VDIM, 2 * HG_KDIM + 2 * HG_VDIM + M_DINNER
    assert z_off % M_DINNER == 0 and xbc_off % M_CONV_DIM == 0
    return pl.pallas_call(
        _ssd_kernel,
        grid=(batch, tiles),
        in_specs=[seq_spec, _resident_cols(D_MODEL, M_DINNER, z_off // M_DINNER),
                  _resident_cols(D_MODEL, M_CONV_DIM, xbc_off // M_CONV_DIM)]
                 + [_resident(a.shape) for a in params],
        out_specs=pl.BlockSpec((SEQ_TILE, M_DINNER), lambda b, j: (b * tiles + j, 0)),
        out_shape=jax.ShapeDtypeStruct((batch * seq, M_DINNER), BF16),
        scratch_shapes=[
            pltpu.VMEM((M_GROUPS, M_STATE, M_GROUP_W), F32),
            pltpu.VMEM((8, M_CONV_DIM), F32),
            pltpu.VMEM((16, M_CONV_DIM), F32),
            pltpu.VMEM((SEQ_TILE, M_CONV_DIM), F32),
            pltpu.VMEM((SEQ_TILE, M_DINNER), F32),
            pltpu.VMEM((SEQ_TILE, M_DINNER), F32),
        ],
        compiler_params=pltpu.CompilerParams(dimension_semantics=("arbitrary", "arbitrary"),
                                             vmem_limit_bytes=VMEM_LIMIT),
        name="ssd",
    )(n, w_all, w_all, *params)


def _merge_kernel(h_ref, n_ref, ohg_ref, y_ref, wbr_ref, whg_ref, wssm_ref, wout_ref, o_ref):
    br = _sigmoid(_dot(n_ref[...].astype(BF16), wbr_ref[...]))
    mixed = (br[:, 0:D_MODEL] * _dot(ohg_ref[...], whg_ref[...])
             + br[:, D_MODEL:2 * D_MODEL] * _dot(y_ref[...], wssm_ref[...]))
    o_ref[...] = h_ref[...] + _dot(mixed.astype(BF16), wout_ref[...])


def _merge(h, n, ohg, y, wbr, whg, wssm, wout):
    t = h.shape[0]
    return pl.pallas_call(
        _merge_kernel,
        grid=(t // ROW_TILE,),
        in_specs=[_row_spec(D_MODEL), _row_spec(D_MODEL), _row_spec(HG_VDIM), _row_spec(M_DINNER),
                  _resident(wbr.shape), _resident(whg.shape), _resident(wssm.shape),
                  _resident(wout.shape)],
        out_specs=_row_spec(D_MODEL),
        out_shape=jax.ShapeDtypeStruct((t, D_MODEL), F32),
        compiler_params=pltpu.CompilerParams(dimension_semantics=("arbitrary",),
                                             vmem_limit_bytes=VMEM_LIMIT),
        name="merge",
    )(h, n, ohg, y, wbr, whg, wssm, wout)


def kernel(x, p, ffn1_norm, ffn1_w13, ffn1_w2, mix_norm, w_in, conv_w, conv_b, dt_bias, a_log,
           d_skip, ssm_norm, hg_lb, hg_norm, w_hg_out, w_ssm_out, w_out, ffn2_norm, ffn2_w13,
           ffn2_w2, ple_norm, w_ple_gate, w_ple_proj, final_norm):
    batch, seq, _ = x.shape
    assert ffn1_w13.shape[0] == 1 and hg_lb.shape[0] == 2, "single-layer block"
    assert seq % SEQ_TILE == 0 and (batch * seq) % ROW_TILE == 0
    t = batch * seq
    row = lambda v: v.reshape(1, -1).astype(F32)
    col = lambda v: v.reshape(-1, 1).astype(F32)
    expand = lambda v: jnp.repeat(v, M_HEADDIM, axis=-1)
    pad_heads = lambda v: jnp.pad(v, [(0, 0)] * (v.ndim - 1) + [(0, _PAIR_W - M_HEADS)])
    even_odd = lambda v: jnp.concatenate([v[..., 0::2], v[..., 1::2]], axis=-1)

    sizes = (HG_KDIM, HG_KDIM, HG_VDIM, HG_VDIM, M_DINNER, M_CONV_DIM, M_HEADS, 2 * D_MODEL)
    offs = [0]
    for s in sizes:
        offs.append(offs[-1] + s)
    w = w_in[0].astype(BF16)
    w_dt = w[:, offs[6]:offs[7]]
    w_br = w[:, offs[7]:offs[8]]
    a_neg = -jnp.exp(a_log[0].astype(F32))
    dtb = dt_bias[0].astype(F32)

    x2 = x.reshape(t, D_MODEL)
    h1, n = _ffn1(x2, row(ffn1_norm[0]), ffn1_w13[0].astype(BF16), ffn1_w2[0].astype(BF16),
                  row(mix_norm[0]))
    o_hg = _hgrn(n, w, hg_lb.astype(F32), row(hg_norm[0]), batch, seq)
    y = _ssd(n, w, pad_heads(w_dt), even_odd(w_dt).T,
             conv_w[0].astype(F32), row(conv_b[0]), row(pad_heads(dtb)), row(pad_heads(a_neg)),
             col(even_odd(dtb)), col(even_odd(a_neg)),
             row(expand(d_skip[0].astype(F32))), row(ssm_norm[0]), batch, seq)
    h2 = _merge(h1, n, o_hg, y, w_br, w_hg_out[0].astype(BF16), w_ssm_out[0].astype(BF16),
                w_out[0].astype(BF16))
    out = _ffn2(h2, row(ffn2_norm[0]), ffn2_w13[0].astype(BF16), ffn2_w2[0].astype(BF16),
                p[0].reshape(t, PLE_DIM), row(ple_norm[0]), w_ple_gate[0].astype(BF16),
                w_ple_proj[0].astype(BF16), row(final_norm))
    return out.reshape(batch, seq, D_MODEL)
```

```python
import jax
import jax.numpy as jnp
from jax import lax
from jax.experimental import pallas as pl
from jax.experimental.pallas import tpu as pltpu

F32 = jnp.float32
BF16 = jnp.bfloat16

D_MODEL = 1024
D_FF = 2816
PLE_DIM = 256
EPS = 1e-6
HG_HEADS = 8
HG_DK = 128
HG_DV = 128
HG_KDIM = HG_HEADS * HG_DK
HG_VDIM = HG_HEADS * HG_DV
HG_BLOCK = 16
M_DINNER = 2048
M_HEADDIM = 64
M_HEADS = 32
M_STATE = 128
M_GROUPS = 4
M_CONV = 4
M_GROUP_W = M_DINNER // M_GROUPS
M_CONV_DIM = M_DINNER + 2 * M_GROUPS * M_STATE
CHUNK = 64

ROW_TILE = 512
SEQ_TILE = 256
N_CHUNKS = SEQ_TILE // CHUNK
COL_BLOCK = 512
VMEM_LIMIT = 56 * 1024 * 1024


def _dot(a, b):
    return jnp.dot(a, b, preferred_element_type=F32)


def _dot_nt(a, b):
    return lax.dot_general(a, b, (((1,), (1,)), ((), ())), preferred_element_type=F32)


def _dot_tn(a, b):
    return lax.dot_general(a, b, (((0,), (0,)), ((), ())), preferred_element_type=F32)


def _split_bf16(x, parts):
    out, rem = [], x
    for _ in range(parts):
        p = rem.astype(BF16)
        out.append(p)
        rem = rem - p.astype(F32)
    return out


def _dot_sel(sel, x, parts=3):
    return _dot(jnp.concatenate([sel] * parts, axis=1), jnp.concatenate(_split_bf16(x, parts), axis=0))


def _dot_sel_r(x, sel, parts=3):
    return _dot(jnp.concatenate(_split_bf16(x, parts), axis=1), jnp.concatenate([sel] * parts, axis=0))


def _rms(x, g):
    ms = jnp.mean(x * x, axis=-1, keepdims=True)
    return x * lax.rsqrt(ms + EPS) * g


def _sigmoid(x):
    return 0.5 * jnp.tanh(0.5 * x) + 0.5


def _silu(x):
    h = 0.5 * x
    return h + h * jnp.tanh(h)


def _softplus(x):
    return jnp.maximum(x, 0.0) + jnp.log(1.0 + jnp.exp(-jnp.abs(x)))


def _bcast_rows(row, rows):
    return jnp.broadcast_to(row, (rows, row.shape[-1]))


def _resident(shape):
    nd = len(shape)
    return pl.BlockSpec(shape, lambda *_: (0,) * nd, pipeline_mode=pl.Buffered(1))


def _resident_cols(rows, width, block):
    return pl.BlockSpec((rows, width), lambda *_: (0, block), pipeline_mode=pl.Buffered(1))


def _swiglu_residual(x, g_ref, w13_ref, w2_ref):
    n = _rms(x, g_ref[...]).astype(BF16)
    gate = _dot(n, w13_ref[:, 0:D_FF])
    up = _dot(n, w13_ref[:, D_FF:2 * D_FF])
    act = (_silu(gate) * up).astype(BF16)
    return x + 0.5 * _dot(act, w2_ref[...])


def _ffn1_kernel(x_ref, g_ref, w13_ref, w2_ref, mixg_ref, h_ref, n_ref):
    h = _swiglu_residual(x_ref[...], g_ref, w13_ref, w2_ref)
    h_ref[...] = h
    n_ref[...] = _rms(h, mixg_ref[...])


def _ffn2_kernel(x_ref, g_ref, w13_ref, w2_ref, p_ref, pleg_ref, wpg_ref, wpp_ref, fing_ref, o_ref):
    h = _swiglu_residual(x_ref[...], g_ref, w13_ref, w2_ref)
    gate = _sigmoid(_dot(_rms(h, pleg_ref[...]).astype(BF16), wpg_ref[...]))
    emb = _dot(p_ref[...].astype(BF16), wpp_ref[...])
    h = h + gate * emb
    o_ref[...] = _rms(h, fing_ref[...])


def _row_spec(width):
    return pl.BlockSpec((ROW_TILE, width), lambda i: (i, 0))


def _ffn1(x, g, w13, w2, mixg):
    t = x.shape[0]
    return pl.pallas_call(
        _ffn1_kernel,
        grid=(t // ROW_TILE,),
        in_specs=[_row_spec(D_MODEL), _resident((1, D_MODEL)), _resident(w13.shape),
                  _resident(w2.shape), _resident((1, D_MODEL))],
        out_specs=[_row_spec(D_MODEL), _row_spec(D_MODEL)],
        out_shape=[jax.ShapeDtypeStruct((t, D_MODEL), F32),
                   jax.ShapeDtypeStruct((t, D_MODEL), F32)],
        compiler_params=pltpu.CompilerParams(dimension_semantics=("arbitrary",),
                                             vmem_limit_bytes=VMEM_LIMIT),
        name="ffn1",
    )(x, g, w13, w2, mixg)


def _ffn2(x, g, w13, w2, p, pleg, wpg, wpp, fing):
    t = x.shape[0]
    return pl.pallas_call(
        _ffn2_kernel,
        grid=(t // ROW_TILE,),
        in_specs=[_row_spec(D_MODEL), _resident((1, D_MODEL)), _resident(w13.shape),
                  _resident(w2.shape), _row_spec(PLE_DIM), _resident((1, D_MODEL)),
                  _resident(wpg.shape), _resident(wpp.shape), _resident((1, D_MODEL))],
        out_specs=_row_spec(D_MODEL),
        out_shape=jax.ShapeDtypeStruct((t, D_MODEL), F32),
        compiler_params=pltpu.CompilerParams(dimension_semantics=("arbitrary",),
                                             vmem_limit_bytes=VMEM_LIMIT),
        name="ffn2",
    )(x, g, w13, w2, p, pleg, wpg, wpp, fing)


def _hgrn_kernel(n_ref, w_ref, lb_ref, gn_ref, o_ref,
                 st_ref, q_ref, k_ref, gl_ref, v_ref, gate_ref, oacc_ref):
    @pl.when(pl.program_id(1) == 0)
    def _():
        st_ref[...] = jnp.zeros_like(st_ref)

    n = n_ref[...].astype(BF16)
    a = lb_ref[...]
    e = jnp.exp(a - jnp.max(a, axis=0, keepdims=True))
    lb = e[0:1] / jnp.sum(e, axis=0, keepdims=True)

    r = lax.broadcasted_iota(jnp.int32, (CHUNK, CHUNK), 0)
    c = lax.broadcasted_iota(jnp.int32, (CHUNK, CHUNK), 1)
    tril_blk = ((c <= r) & (c // HG_BLOCK == r // HG_BLOCK)).astype(BF16)
    f_raws = [_dot(n, w_ref[:, HG_KDIM + cb * COL_BLOCK:HG_KDIM + (cb + 1) * COL_BLOCK])
              for cb in range(HG_KDIM // COL_BLOCK)]
    q_ref[...] = _dot(n, w_ref[:, 0:HG_KDIM]) * (HG_DK ** -0.5)
    v_ref[...] = _dot(n, w_ref[:, 2 * HG_KDIM:2 * HG_KDIM + HG_VDIM]).astype(BF16)
    for cb, f_raw in enumerate(f_raws):
        cols = slice(cb * COL_BLOCK, (cb + 1) * COL_BLOCK)
        f = lb[:, cols] + (1.0 - lb[:, cols]) * _sigmoid(f_raw)
        k_ref[:, cols] = 1.0 - f
        logf = jnp.log(f)
        for ci in range(N_CHUNKS):
            rows = slice(ci * CHUNK, (ci + 1) * CHUNK)
            gl_ref[rows, cols] = _dot_sel(tril_blk, logf[rows])

    t_i = lax.broadcasted_iota(jnp.int32, (CHUNK, 4 * CHUNK), 0)
    c_i = lax.broadcasted_iota(jnp.int32, (CHUNK, 4 * CHUNK), 1)
    s_i, grp = c_i % CHUNK, c_i // CHUNK
    bt, bs = t_i // HG_BLOCK, s_i // HG_BLOCK
    m_all = (((grp == 0) & (bt == bs) & (s_i <= t_i))
             | ((grp == 1) & (bt % 2 == 1) & (bs == bt - 1))
             | ((grp == 2) & (bt >= 2) & (bs < 2)))
    zero_blk = jnp.zeros((CHUNK, HG_DK), BF16)
    ones_blk = jnp.ones((HG_BLOCK, HG_KDIM), F32)
    gate_cols = HG_VDIM // N_CHUNKS

    atts = {}
    prepped = []
    for ci in range(N_CHUNKS):
        rows = slice(ci * CHUNK, (ci + 1) * CHUNK)
        q, k, gl, v = q_ref[rows, :], k_ref[rows, :], gl_ref[rows, :], v_ref[rows, :]
        tot = [gl[HG_BLOCK * i + HG_BLOCK - 1:HG_BLOCK * (i + 1), :] for i in range(4)]
        e_t = [jnp.exp(t) for t in tot]
        e01, e23 = e_t[0] * e_t[1], e_t[2] * e_t[3]
        dec = e01 * e23
        tot_rows = jnp.concatenate([_bcast_rows(t, HG_BLOCK) for t in tot], axis=0)
        qd = q * jnp.exp(gl)
        ki = k * jnp.exp(-gl)
        ke = k * jnp.exp(tot_rows - gl)
        qd32 = qd * jnp.concatenate(
            [ones_blk, _bcast_rows(e_t[0], HG_BLOCK), ones_blk, _bcast_rows(e_t[2], HG_BLOCK)], axis=0)
        ke32 = ke * jnp.concatenate(
            [_bcast_rows(e_t[1], HG_BLOCK), ones_blk, _bcast_rows(e_t[3], HG_BLOCK), ones_blk], axis=0)
        qd64 = qd32 * jnp.concatenate(
            [ones_blk, ones_blk, _bcast_rows(e01, HG_BLOCK), _bcast_rows(e01, HG_BLOCK)], axis=0)
        ke64 = ke32 * jnp.concatenate(
            [_bcast_rows(e23, HG_BLOCK), _bcast_rows(e23, HG_BLOCK), ones_blk, ones_blk], axis=0)
        qd, ki, ke, qd32, ke32, qd64, ke64 = (
            t.astype(BF16) for t in (qd, ki, ke, qd32, ke32, qd64, ke64))
        prepped.append((v, dec, qd64, ke64))
        g0 = 2 * HG_KDIM + HG_VDIM + ci * gate_cols
        gate_ref[:, ci * gate_cols:(ci + 1) * gate_cols] = _silu(_dot(n, w_ref[:, g0:g0 + gate_cols]))
        for h in range(HG_HEADS):
            kl = slice(h * HG_DK, (h + 1) * HG_DK)
            lhs = jnp.concatenate([qd[:, kl], qd32[:, kl]], axis=1)
            rhs_t = jnp.concatenate([jnp.concatenate([ki[:, kl], zero_blk], axis=1),
                                     jnp.concatenate([ke[:, kl], zero_blk], axis=1),
                                     jnp.concatenate([zero_blk, ke32[:, kl]], axis=1),
                                     jnp.concatenate([zero_blk, zero_blk], axis=1)], axis=0)
            atts[ci, h] = jnp.where(m_all, _dot_nt(lhs, rhs_t), 0.0).astype(BF16)
    gn = gn_ref[...]
    first_head = lax.broadcasted_iota(jnp.int32, (CHUNK, 2 * HG_DV), 1) < HG_DV
    for ci in range(N_CHUNKS):
        rows = slice(ci * CHUNK, (ci + 1) * CHUNK)
        v, dec, qd64, ke64 = prepped[ci]
        for h in range(0, HG_HEADS, 2):
            kl2 = slice(h * HG_DK, (h + 2) * HG_DK)
            vl2 = slice(h * HG_DV, (h + 2) * HG_DV)
            st_a, st_b = st_ref[h], st_ref[h + 1]
            v2 = v[:, vl2]
            zero_v = jnp.zeros_like(v2)
            v_a = jnp.where(first_head, v2, zero_v)
            v_b = jnp.where(first_head, zero_v, v2)
            v_bd = jnp.concatenate([v_a, v_a, v_a, zero_v, v_b, v_b, v_b, zero_v], axis=0)
            zero_s = jnp.zeros((HG_DV, HG_DK), BF16)
            st_bd = jnp.concatenate([jnp.concatenate([st_a.astype(BF16), zero_s], axis=1),
                                     jnp.concatenate([zero_s, st_b.astype(BF16)], axis=1)], axis=0)
            att2 = jnp.concatenate([atts[ci, h], atts[ci, h + 1]], axis=1)
            oacc_ref[rows, vl2] = _dot(att2, v_bd) + _dot_nt(qd64[:, kl2], st_bd)
            incr = _dot_tn(v2, ke64[:, kl2])
            st_ref[h] = st_a * dec[:, h * HG_DK:(h + 1) * HG_DK] + incr[0:HG_DV, 0:HG_DK]
            st_ref[h + 1] = (st_b * dec[:, (h + 1) * HG_DK:(h + 2) * HG_DK]
                             + incr[HG_DV:2 * HG_DV, HG_DK:2 * HG_DK])
        for h in range(HG_HEADS):
            vl = slice(h * HG_DV, (h + 1) * HG_DV)
            o_ref[rows, vl] = (_rms(oacc_ref[rows, vl], gn[:, vl]) * gate_ref[rows, vl]).astype(BF16)


def _hgrn(n, w, hg_lb, hg_norm, batch, seq):
    tiles = seq // SEQ_TILE
    seq_spec = pl.BlockSpec((SEQ_TILE, D_MODEL), lambda b, j: (b * tiles + j, 0))
    return pl.pallas_call(
        _hgrn_kernel,
        grid=(batch, tiles),
        in_specs=[seq_spec, _resident_cols(D_MODEL, 2 * HG_KDIM + 2 * HG_VDIM, 0), _resident(hg_lb.shape),
                  _resident((1, HG_VDIM))],
        out_specs=pl.BlockSpec((SEQ_TILE, HG_VDIM), lambda b, j: (b * tiles + j, 0)),
        out_shape=jax.ShapeDtypeStruct((batch * seq, HG_VDIM), BF16),
        scratch_shapes=[
            pltpu.VMEM((HG_HEADS, HG_DV, HG_DK), F32),
            pltpu.VMEM((SEQ_TILE, HG_KDIM), F32),
            pltpu.VMEM((SEQ_TILE, HG_KDIM), F32),
            pltpu.VMEM((SEQ_TILE, HG_KDIM), F32),
            pltpu.VMEM((SEQ_TILE, HG_VDIM), BF16),
            pltpu.VMEM((SEQ_TILE, HG_VDIM), F32),
            pltpu.VMEM((SEQ_TILE, HG_VDIM), F32),
        ],
        compiler_params=pltpu.CompilerParams(dimension_semantics=("arbitrary", "arbitrary"),
                                             vmem_limit_bytes=VMEM_LIMIT),
        name="hgrn2",
    )(n, w, hg_lb, hg_norm)


_PAIRS = M_HEADS // 2
_PAIR_W = 2 * M_HEADDIM


def _ssd_kernel(n_ref, wz_ref, wxbc_ref, wdt_ref, cw_ref, cb_ref, dtb_ref, aneg_ref,
                dsk_ref, gn_ref, y_ref,
                st_ref, hist_ref, edge_ref, xbc_ref, zs_ref, yacc_ref):
    @pl.when(pl.program_id(1) == 0)
    def _():
        st_ref[...] = jnp.zeros_like(st_ref)
        hist_ref[...] = jnp.zeros_like(hist_ref)

    n = n_ref[...].astype(BF16)
    r = lax.broadcasted_iota(jnp.int32, (CHUNK, CHUNK), 0)
    c = lax.broadcasted_iota(jnp.int32, (CHUNK, CHUNK), 1)
    tril_chunk = (c <= r).astype(BF16)
    e_r = lax.broadcasted_iota(jnp.int32, (_PAIR_W, COL_BLOCK), 0)
    e_c = lax.broadcasted_iota(jnp.int32, (_PAIR_W, COL_BLOCK), 1)
    lane_head = jnp.where(e_r < _PAIRS, 2 * e_r, jnp.where(e_r < M_HEADS, 2 * (e_r - _PAIRS) + 1, -1))

    dt_c = _softplus(_dot(n, wdt_ref[...]) + dtb_ref[...])
    a_c = dt_c * aneg_ref[...]
    acs_c = jnp.concatenate([_dot_sel(tril_chunk, a_c[ci * CHUNK:(ci + 1) * CHUNK])
                             for ci in range(N_CHUNKS)], axis=0)
    dt_parts = jnp.concatenate(_split_bf16(dt_c, 2), axis=1)
    acs_parts = jnp.concatenate(_split_bf16(acs_c, 3), axis=1)
    acs_t = acs_c.T[0:M_HEADS]

    first_expand = (M_CONV_DIM - M_DINNER) // COL_BLOCK
    dt_blocks, acs_blocks = [], []
    for cb_i in range(M_CONV_DIM // COL_BLOCK):
        cols = slice(cb_i * COL_BLOCK, (cb_i + 1) * COL_BLOCK)
        x = _dot(n, wxbc_ref[:, cols])
        cw = cw_ref[:, cols]
        cb = cb_ref[:, cols]
        x1 = pltpu.roll(x, 1, 0)
        near = x * cw[3:4] + x1 * cw[2:3]
        far = x * cw[1:2] + x1 * cw[0:1]
        xbc_ref[:, cols] = _silu(near + pltpu.roll(far, 2, 0) + cb)
        edge_ref[0:8, cols] = hist_ref[:, cols]
        edge_ref[8:16, cols] = x[0:8]
        top = _bcast_rows(cb, 8)
        for kk in range(M_CONV):
            top = top + edge_ref[8 - (M_CONV - 1) + kk:16 - (M_CONV - 1) + kk, cols] * cw[kk:kk + 1]
        xbc_ref[0:8, cols] = _silu(top)
        hist_ref[:, cols] = x[SEQ_TILE - 8:SEQ_TILE]
        if cb_i >= first_expand:
            expand = ((e_c + (cb_i - first_expand) * COL_BLOCK) // M_HEADDIM == lane_head).astype(BF16)
            dt_blocks.append(_dot(dt_parts, jnp.concatenate([expand] * 2, axis=0)))
            acs_blocks.append(_dot(acs_parts, jnp.concatenate([expand] * 3, axis=0)))
    dt_e = jnp.concatenate(dt_blocks, axis=1)
    acs_e = jnp.concatenate(acs_blocks, axis=1)

    lane_left = lax.broadcasted_iota(jnp.int32, (_PAIRS, _PAIR_W), 1) < CHUNK
    acs_pair = []
    for vc in range(SEQ_TILE // _PAIR_W):
        ev = acs_t[0:_PAIRS, vc * _PAIR_W:(vc + 1) * _PAIR_W]
        od = acs_t[_PAIRS:2 * _PAIRS, vc * _PAIR_W:(vc + 1) * _PAIR_W]
        acs_pair.append(jnp.where(lane_left, ev, pltpu.roll(od, CHUNK, 1)))
        acs_pair.append(jnp.where(lane_left, pltpu.roll(ev, CHUNK, 1), od))

    xdt = xbc_ref[:, 0:M_DINNER] * dt_e
    w_start = jnp.exp(acs_e)

    t_i = lax.broadcasted_iota(jnp.int32, (CHUNK, _PAIR_W), 0)
    l_i = lax.broadcasted_iota(jnp.int32, (CHUNK, _PAIR_W), 1)
    causal2 = (l_i % CHUNK) <= t_i
    left = l_i < M_HEADDIM

    row_chunk = lax.broadcasted_iota(jnp.int32, (SEQ_TILE, M_STATE), 0) // CHUNK
    acs_last = [acs_e[(ci + 1) * CHUNK - 1:(ci + 1) * CHUNK] for ci in range(N_CHUNKS)]
    acs_last_rows = jnp.concatenate([_bcast_rows(a, CHUNK) for a in acs_last], axis=0)
    xw = (xdt * jnp.exp(acs_last_rows - acs_e)).astype(BF16)
    xdt_b = xdt.astype(BF16)

    def by_chunk(v):
        zero = jnp.zeros_like(v)
        return jnp.concatenate([jnp.where(row_chunk == ci, v, zero) for ci in range(N_CHUNKS)], axis=1)

    for g in range(M_GROUPS):
        gl = slice(g * M_GROUP_W, (g + 1) * M_GROUP_W)
        b_g = xbc_ref[:, M_DINNER + g * M_STATE:M_DINNER + (g + 1) * M_STATE].astype(BF16)
        c_g = xbc_ref[:, M_DINNER + (M_GROUPS + g) * M_STATE:
                      M_DINNER + (M_GROUPS + g + 1) * M_STATE].astype(BF16)
        zs_ref[:, gl] = _silu(_dot(n, wz_ref[:, gl]))
        b_dup = jnp.concatenate([b_g[ci * CHUNK:(ci + 1) * CHUNK] for ci in range(N_CHUNKS) for _ in (0, 1)],
                                axis=0)
        cb_all = _dot_nt(c_g, b_dup)
        for ci in range(N_CHUNKS):
            rows = slice(ci * CHUNK, (ci + 1) * CHUNK)
            cb2 = cb_all[rows, ci * _PAIR_W:(ci + 1) * _PAIR_W]
            for qq in range(M_GROUP_W // (2 * _PAIR_W)):
                ms, rhss = [], []
                for pp in (2 * qq, 2 * qq + 1):
                    p = g * (M_GROUP_W // _PAIR_W) + pp
                    pl_ = slice(p * _PAIR_W, (p + 1) * _PAIR_W)
                    seg = acs_e[rows, pl_] - acs_pair[ci][p:p + 1, :]
                    decay = jnp.exp(jnp.where(causal2, seg, -jnp.inf))
                    ms.append((cb2 * decay).astype(BF16))
                    xp = xdt_b[rows, pl_]
                    zero = jnp.zeros_like(xp)
                    rhss.append(jnp.concatenate([jnp.where(left, xp, zero), jnp.where(left, zero, xp)], axis=0))
                zblk = jnp.zeros_like(rhss[0])
                rhs = jnp.concatenate([jnp.concatenate([rhss[0], zblk], axis=1),
                                       jnp.concatenate([zblk, rhss[1]], axis=1)], axis=0)
                q0 = (g * (M_GROUP_W // _PAIR_W) + 2 * qq) * _PAIR_W
                yacc_ref[rows, q0:q0 + 2 * _PAIR_W] = _dot(jnp.concatenate(ms, axis=1), rhs)
        incr = _dot_tn(by_chunk(b_g), xw[:, gl])
        st = st_ref[g]
        starts = []
        for ci in range(N_CHUNKS):
            starts.append(st.astype(BF16))
            st = st * jnp.exp(acs_last[ci][:, gl]) + incr[ci * M_STATE:(ci + 1) * M_STATE]
        st_ref[g] = st
        y_inter = _dot(by_chunk(c_g), jnp.concatenate(starts, axis=0))
        y = (yacc_ref[:, gl] + y_inter * w_start[:, gl] + dsk_ref[:, gl] * xbc_ref[:, gl]) * zs_ref[:, gl]
        y_ref[:, gl] = _rms(y, gn_ref[:, gl]).astype(BF16)


def _ssd(n, w_all, wdt, cw, cb, dtb, aneg, dsk, gn, batch, seq):
    tiles = seq // SEQ_TILE
    seq_spec = pl.BlockSpec((SEQ_TILE, D_MODEL), lambda b, j: (b * tiles + j, 0))
    params = (wdt, cw, cb, dtb, aneg, dsk, gn)
    z_off, xbc_off = 2 * HG_KDIM + 2 * HG_VDIM, 2 * HG_KDIM + 2 * HG_VDIM + M_DINNER
    assert z_off % M_DINNER == 0 and xbc_off % M_CONV_DIM == 0
    return pl.pallas_call(
        _ssd_kernel,
        grid=(batch, tiles),
        in_specs=[seq_spec, _resident_cols(D_MODEL, M_DINNER, z_off // M_DINNER),
                  _resident_cols(D_MODEL, M_CONV_DIM, xbc_off // M_CONV_DIM)]
                 + [_resident(a.shape) for a in params],
        out_specs=pl.BlockSpec((SEQ_TILE, M_DINNER), lambda b, j: (b * tiles + j, 0)),
        out_shape=jax.ShapeDtypeStruct((batch * seq, M_DINNER), BF16),
        scratch_shapes=[
            pltpu.VMEM((M_GROUPS, M_STATE, M_GROUP_W), F32),
            pltpu.VMEM((8, M_CONV_DIM), F32),
            pltpu.VMEM((16, M_CONV_DIM), F32),
            pltpu.VMEM((SEQ_TILE, M_CONV_DIM), F32),
            pltpu.VMEM((SEQ_TILE, M_DINNER), F32),
            pltpu.VMEM((SEQ_TILE, M_DINNER), F32),
        ],
        compiler_params=pltpu.CompilerParams(dimension_semantics=("arbitrary", "arbitrary"),
                                             vmem_limit_bytes=VMEM_LIMIT),
        name="ssd",
    )(n, w_all, w_all, *params)


def _merge_kernel(h_ref, n_ref, ohg_ref, y_ref, wbr_ref, whg_ref, wssm_ref, wout_ref, o_ref):
    br = _sigmoid(_dot(n_ref[...].astype(BF16), wbr_ref[...]))
    mixed = (br[:, 0:D_MODEL] * _dot(ohg_ref[...], whg_ref[...])
             + br[:, D_MODEL:2 * D_MODEL] * _dot(y_ref[...], wssm_ref[...]))
    o_ref[...] = h_ref[...] + _dot(mixed.astype(BF16), wout_ref[...])


def _merge(h, n, ohg, y, wbr, whg, wssm, wout):
    t = h.shape[0]
    return pl.pallas_call(
        _merge_kernel,
        grid=(t // ROW_TILE,),
        in_specs=[_row_spec(D_MODEL), _row_spec(D_MODEL), _row_spec(HG_VDIM), _row_spec(M_DINNER),
                  _resident(wbr.shape), _resident(whg.shape), _resident(wssm.shape),
                  _resident(wout.shape)],
        out_specs=_row_spec(D_MODEL),
        out_shape=jax.ShapeDtypeStruct((t, D_MODEL), F32),
        compiler_params=pltpu.CompilerParams(dimension_semantics=("arbitrary",),
                                             vmem_limit_bytes=VMEM_LIMIT),
        name="merge",
    )(h, n, ohg, y, wbr, whg, wssm, wout)


def kernel(x, p, ffn1_norm, ffn1_w13, ffn1_w2, mix_norm, w_in, conv_w, conv_b, dt_bias, a_log,
           d_skip, ssm_norm, hg_lb, hg_norm, w_hg_out, w_ssm_out, w_out, ffn2_norm, ffn2_w13,
           ffn2_w2, ple_norm, w_ple_gate, w_ple_proj, final_norm):
    batch, seq, _ = x.shape
    assert ffn1_w13.shape[0] == 1 and hg_lb.shape[0] == 2, "single-layer block"
    assert seq % SEQ_TILE == 0 and (batch * seq) % ROW_TILE == 0
    t = batch * seq
    row = lambda v: v.reshape(1, -1).astype(F32)
    expand = lambda v: jnp.repeat(v, M_HEADDIM, axis=-1)
    pad_heads = lambda v: jnp.pad(v, [(0, 0)] * (v.ndim - 1) + [(0, _PAIR_W - M_HEADS)])
    even_odd = lambda v: jnp.concatenate([v[..., 0::2], v[..., 1::2]], axis=-1)

    sizes = (HG_KDIM, HG_KDIM, HG_VDIM, HG_VDIM, M_DINNER, M_CONV_DIM, M_HEADS, 2 * D_MODEL)
    offs = [0]
    for s in sizes:
        offs.append(offs[-1] + s)
    w = w_in[0].astype(BF16)
    w_dt = w[:, offs[6]:offs[7]]
    w_br = w[:, offs[7]:offs[8]]
    a_neg = -jnp.exp(a_log[0].astype(F32))
    dtb = dt_bias[0].astype(F32)

    x2 = x.reshape(t, D_MODEL)
    h1, n = _ffn1(x2, row(ffn1_norm[0]), ffn1_w13[0].astype(BF16), ffn1_w2[0].astype(BF16),
                  row(mix_norm[0]))
    o_hg = _hgrn(n, w, hg_lb.astype(F32), row(hg_norm[0]), batch, seq)
    y = _ssd(n, w, pad_heads(even_odd(w_dt)),
             conv_w[0].astype(F32), row(conv_b[0]), row(pad_heads(even_odd(dtb))),
             row(pad_heads(even_odd(a_neg))),
             row(expand(d_skip[0].astype(F32))), row(ssm_norm[0]), batch, seq)
    h2 = _merge(h1, n, o_hg, y, w_br, w_hg_out[0].astype(BF16), w_ssm_out[0].astype(BF16),
                w_out[0].astype(BF16))
    out = _ffn2(h2, row(ffn2_norm[0]), ffn2_w13[0].astype(BF16), ffn2_w2[0].astype(BF16),
                p[0].reshape(t, PLE_DIM), row(ple_norm[0]), w_ple_gate[0].astype(BF16),
                w_ple_proj[0].astype(BF16), row(final_norm))
    return out.reshape(batch, seq, D_MODEL)
```

```python
import jax
import jax.numpy as jnp
from jax import lax
from jax.experimental import pallas as pl
from jax.experimental.pallas import tpu as pltpu

F32 = jnp.float32
BF16 = jnp.bfloat16

D_MODEL = 1024
D_FF = 2816
PLE_DIM = 256
EPS = 1e-6
HG_HEADS = 8
HG_DK = 128
HG_DV = 128
HG_KDIM = HG_HEADS * HG_DK
HG_VDIM = HG_HEADS * HG_DV
HG_BLOCK = 16
M_DINNER = 2048
M_HEADDIM = 64
M_HEADS = 32
M_STATE = 128
M_GROUPS = 4
M_CONV = 4
M_GROUP_W = M_DINNER // M_GROUPS
M_CONV_DIM = M_DINNER + 2 * M_GROUPS * M_STATE
CHUNK = 64

ROW_TILE = 512
SEQ_TILE = 256
N_CHUNKS = SEQ_TILE // CHUNK
COL_BLOCK = 512
VMEM_LIMIT = 56 * 1024 * 1024


def _dot(a, b):
    return jnp.dot(a, b, preferred_element_type=F32)


def _dot_nt(a, b):
    return lax.dot_general(a, b, (((1,), (1,)), ((), ())), preferred_element_type=F32)


def _dot_tn(a, b):
    return lax.dot_general(a, b, (((0,), (0,)), ((), ())), preferred_element_type=F32)


def _split_bf16(x, parts):
    out, rem = [], x
    for _ in range(parts):
        p = rem.astype(BF16)
        out.append(p)
        rem = rem - p.astype(F32)
    return out


def _dot_sel(sel, x, parts=3):
    return _dot(jnp.concatenate([sel] * parts, axis=1), jnp.concatenate(_split_bf16(x, parts), axis=0))


def _dot_sel_r(x, sel, parts=3):
    return _dot(jnp.concatenate(_split_bf16(x, parts), axis=1), jnp.concatenate([sel] * parts, axis=0))


def _rms(x, g):
    ms = jnp.mean(x * x, axis=-1, keepdims=True)
    return x * lax.rsqrt(ms + EPS) * g


def _sigmoid(x):
    return 0.5 * jnp.tanh(0.5 * x) + 0.5


def _silu(x):
    h = 0.5 * x
    return h + h * jnp.tanh(h)


def _softplus(x):
    return jnp.maximum(x, 0.0) + jnp.log(1.0 + jnp.exp(-jnp.abs(x)))


def _bcast_rows(row, rows):
    return jnp.broadcast_to(row, (rows, row.shape[-1]))


def _resident(shape):
    nd = len(shape)
    return pl.BlockSpec(shape, lambda *_: (0,) * nd, pipeline_mode=pl.Buffered(1))


def _resident_cols(rows, width, block):
    return pl.BlockSpec((rows, width), lambda *_: (0, block), pipeline_mode=pl.Buffered(1))


def _swiglu_residual(x, g_ref, w13_ref, w2_ref):
    n = _rms(x, g_ref[...]).astype(BF16)
    gate = _dot(n, w13_ref[:, 0:D_FF])
    up = _dot(n, w13_ref[:, D_FF:2 * D_FF])
    act = (_silu(gate) * up).astype(BF16)
    return x + 0.5 * _dot(act, w2_ref[...])


def _ffn1_kernel(x_ref, g_ref, w13_ref, w2_ref, mixg_ref, h_ref, n_ref):
    h = _swiglu_residual(x_ref[...], g_ref, w13_ref, w2_ref)
    h_ref[...] = h
    n_ref[...] = _rms(h, mixg_ref[...])


def _ffn2_kernel(x_ref, g_ref, w13_ref, w2_ref, p_ref, pleg_ref, wpg_ref, wpp_ref, fing_ref, o_ref):
    h = _swiglu_residual(x_ref[...], g_ref, w13_ref, w2_ref)
    gate = _sigmoid(_dot(_rms(h, pleg_ref[...]).astype(BF16), wpg_ref[...]))
    emb = _dot(p_ref[...].astype(BF16), wpp_ref[...])
    h = h + gate * emb
    o_ref[...] = _rms(h, fing_ref[...])


def _row_spec(width):
    return pl.BlockSpec((ROW_TILE, width), lambda i: (i, 0))


def _ffn1(x, g, w13, w2, mixg):
    t = x.shape[0]
    return pl.pallas_call(
        _ffn1_kernel,
        grid=(t // ROW_TILE,),
        in_specs=[_row_spec(D_MODEL), _resident((1, D_MODEL)), _resident(w13.shape),
                  _resident(w2.shape), _resident((1, D_MODEL))],
        out_specs=[_row_spec(D_MODEL), _row_spec(D_MODEL)],
        out_shape=[jax.ShapeDtypeStruct((t, D_MODEL), F32),
                   jax.ShapeDtypeStruct((t, D_MODEL), F32)],
        compiler_params=pltpu.CompilerParams(dimension_semantics=("arbitrary",),
                                             vmem_limit_bytes=VMEM_LIMIT),
        name="ffn1",
    )(x, g, w13, w2, mixg)


def _ffn2(x, g, w13, w2, p, pleg, wpg, wpp, fing):
    t = x.shape[0]
    return pl.pallas_call(
        _ffn2_kernel,
        grid=(t // ROW_TILE,),
        in_specs=[_row_spec(D_MODEL), _resident((1, D_MODEL)), _resident(w13.shape),
                  _resident(w2.shape), _row_spec(PLE_DIM), _resident((1, D_MODEL)),
                  _resident(wpg.shape), _resident(wpp.shape), _resident((1, D_MODEL))],
        out_specs=_row_spec(D_MODEL),
        out_shape=jax.ShapeDtypeStruct((t, D_MODEL), F32),
        compiler_params=pltpu.CompilerParams(dimension_semantics=("arbitrary",),
                                             vmem_limit_bytes=VMEM_LIMIT),
        name="ffn2",
    )(x, g, w13, w2, p, pleg, wpg, wpp, fing)


def _hgrn_kernel(n_ref, w_ref, lb_ref, gn_ref, o_ref,
                 st_ref, q_ref, k_ref, gl_ref, v_ref, gate_ref, oacc_ref):
    @pl.when(pl.program_id(1) == 0)
    def _():
        st_ref[...] = jnp.zeros_like(st_ref)

    n = n_ref[...].astype(BF16)
    a = lb_ref[...]
    e = jnp.exp(a - jnp.max(a, axis=0, keepdims=True))
    lb = e[0:1] / jnp.sum(e, axis=0, keepdims=True)

    r = lax.broadcasted_iota(jnp.int32, (CHUNK, CHUNK), 0)
    c = lax.broadcasted_iota(jnp.int32, (CHUNK, CHUNK), 1)
    tril_blk = ((c <= r) & (c // HG_BLOCK == r // HG_BLOCK)).astype(BF16)
    f_raws = [_dot(n, w_ref[:, HG_KDIM + cb * COL_BLOCK:HG_KDIM + (cb + 1) * COL_BLOCK])
              for cb in range(HG_KDIM // COL_BLOCK)]
    q_ref[...] = _dot(n, w_ref[:, 0:HG_KDIM]) * (HG_DK ** -0.5)
    v_ref[...] = _dot(n, w_ref[:, 2 * HG_KDIM:2 * HG_KDIM + HG_VDIM]).astype(BF16)
    for cb, f_raw in enumerate(f_raws):
        cols = slice(cb * COL_BLOCK, (cb + 1) * COL_BLOCK)
        f = lb[:, cols] + (1.0 - lb[:, cols]) * _sigmoid(f_raw)
        k_ref[:, cols] = 1.0 - f
        logf = jnp.log(f)
        for ci in range(N_CHUNKS):
            rows = slice(ci * CHUNK, (ci + 1) * CHUNK)
            gl_ref[rows, cols] = _dot_sel(tril_blk, logf[rows])

    t_i = lax.broadcasted_iota(jnp.int32, (CHUNK, 4 * CHUNK), 0)
    c_i = lax.broadcasted_iota(jnp.int32, (CHUNK, 4 * CHUNK), 1)
    s_i, grp = c_i % CHUNK, c_i // CHUNK
    bt, bs = t_i // HG_BLOCK, s_i // HG_BLOCK
    m_all = (((grp == 0) & (bt == bs) & (s_i <= t_i))
             | ((grp == 1) & (bt % 2 == 1) & (bs == bt - 1))
             | ((grp == 2) & (bt >= 2) & (bs < 2)))
    zero_blk = jnp.zeros((CHUNK, HG_DK), BF16)
    ones_blk = jnp.ones((HG_BLOCK, HG_KDIM), F32)
    gate_cols = HG_VDIM // N_CHUNKS

    atts = {}
    prepped = []
    for ci in range(N_CHUNKS):
        rows = slice(ci * CHUNK, (ci + 1) * CHUNK)
        q, k, gl, v = q_ref[rows, :], k_ref[rows, :], gl_ref[rows, :], v_ref[rows, :]
        tot = [gl[HG_BLOCK * i + HG_BLOCK - 1:HG_BLOCK * (i + 1), :] for i in range(4)]
        e_t = [jnp.exp(t) for t in tot]
        e01, e23 = e_t[0] * e_t[1], e_t[2] * e_t[3]
        dec = e01 * e23
        tot_rows = jnp.concatenate([_bcast_rows(t, HG_BLOCK) for t in tot], axis=0)
        qd = q * jnp.exp(gl)
        ki = k * jnp.exp(-gl)
        ke = k * jnp.exp(tot_rows - gl)
        qd32 = qd * jnp.concatenate(
            [ones_blk, _bcast_rows(e_t[0], HG_BLOCK), ones_blk, _bcast_rows(e_t[2], HG_BLOCK)], axis=0)
        ke32 = ke * jnp.concatenate(
            [_bcast_rows(e_t[1], HG_BLOCK), ones_blk, _bcast_rows(e_t[3], HG_BLOCK), ones_blk], axis=0)
        qd64 = qd32 * jnp.concatenate(
            [ones_blk, ones_blk, _bcast_rows(e01, HG_BLOCK), _bcast_rows(e01, HG_BLOCK)], axis=0)
        ke64 = ke32 * jnp.concatenate(
            [_bcast_rows(e23, HG_BLOCK), _bcast_rows(e23, HG_BLOCK), ones_blk, ones_blk], axis=0)
        qd, ki, ke, qd32, ke32, qd64, ke64 = (
            t.astype(BF16) for t in (qd, ki, ke, qd32, ke32, qd64, ke64))
        prepped.append((v, dec, qd64, ke64))
        g0 = 2 * HG_KDIM + HG_VDIM + ci * gate_cols
        gate_ref[:, ci * gate_cols:(ci + 1) * gate_cols] = _silu(_dot(n, w_ref[:, g0:g0 + gate_cols]))
        for h in range(HG_HEADS):
            kl = slice(h * HG_DK, (h + 1) * HG_DK)
            lhs = jnp.concatenate([qd[:, kl], qd32[:, kl]], axis=1)
            rhs_t = jnp.concatenate([jnp.concatenate([ki[:, kl], zero_blk], axis=1),
                                     jnp.concatenate([ke[:, kl], zero_blk], axis=1),
                                     jnp.concatenate([zero_blk, ke32[:, kl]], axis=1),
                                     jnp.concatenate([zero_blk, zero_blk], axis=1)], axis=0)
            atts[ci, h] = jnp.where(m_all, _dot_nt(lhs, rhs_t), 0.0).astype(BF16)
    gn = gn_ref[...]
    first_head = lax.broadcasted_iota(jnp.int32, (CHUNK, 2 * HG_DV), 1) < HG_DV
    for ci in range(N_CHUNKS):
        rows = slice(ci * CHUNK, (ci + 1) * CHUNK)
        v, dec, qd64, ke64 = prepped[ci]
        for h in range(0, HG_HEADS, 2):
            kl2 = slice(h * HG_DK, (h + 2) * HG_DK)
            vl2 = slice(h * HG_DV, (h + 2) * HG_DV)
            st_a, st_b = st_ref[h], st_ref[h + 1]
            v2 = v[:, vl2]
            zero_v = jnp.zeros_like(v2)
            v_a = jnp.where(first_head, v2, zero_v)
            v_b = jnp.where(first_head, zero_v, v2)
            v_bd = jnp.concatenate([v_a, v_a, v_a, zero_v, v_b, v_b, v_b, zero_v], axis=0)
            zero_s = jnp.zeros((HG_DV, HG_DK), BF16)
            st_bd = jnp.concatenate([jnp.concatenate([st_a.astype(BF16), zero_s], axis=1),
                                     jnp.concatenate([zero_s, st_b.astype(BF16)], axis=1)], axis=0)
            att2 = jnp.concatenate([atts[ci, h], atts[ci, h + 1]], axis=1)
            oacc_ref[rows, vl2] = _dot(att2, v_bd) + _dot_nt(qd64[:, kl2], st_bd)
            incr = _dot_tn(v2, ke64[:, kl2])
            st_ref[h] = st_a * dec[:, h * HG_DK:(h + 1) * HG_DK] + incr[0:HG_DV, 0:HG_DK]
            st_ref[h + 1] = (st_b * dec[:, (h + 1) * HG_DK:(h + 2) * HG_DK]
                             + incr[HG_DV:2 * HG_DV, HG_DK:2 * HG_DK])
        for h in range(HG_HEADS):
            vl = slice(h * HG_DV, (h + 1) * HG_DV)
            o_ref[rows, vl] = (_rms(oacc_ref[rows, vl], gn[:, vl]) * gate_ref[rows, vl]).astype(BF16)


def _hgrn(n, w, hg_lb, hg_norm, batch, seq):
    tiles = seq // SEQ_TILE
    seq_spec = pl.BlockSpec((SEQ_TILE, D_MODEL), lambda b, j: (b * tiles + j, 0))
    return pl.pallas_call(
        _hgrn_kernel,
        grid=(batch, tiles),
        in_specs=[seq_spec, _resident_cols(D_MODEL, 2 * HG_KDIM + 2 * HG_VDIM, 0), _resident(hg_lb.shape),
                  _resident((1, HG_VDIM))],
        out_specs=pl.BlockSpec((SEQ_TILE, HG_VDIM), lambda b, j: (b * tiles + j, 0)),
        out_shape=jax.ShapeDtypeStruct((batch * seq, HG_VDIM), BF16),
        scratch_shapes=[
            pltpu.VMEM((HG_HEADS, HG_DV, HG_DK), F32),
            pltpu.VMEM((SEQ_TILE, HG_KDIM), F32),
            pltpu.VMEM((SEQ_TILE, HG_KDIM), F32),
            pltpu.VMEM((SEQ_TILE, HG_KDIM), F32),
            pltpu.VMEM((SEQ_TILE, HG_VDIM), BF16),
            pltpu.VMEM((SEQ_TILE, HG_VDIM), F32),
            pltpu.VMEM((SEQ_TILE, HG_VDIM), F32),
        ],
        compiler_params=pltpu.CompilerParams(dimension_semantics=("arbitrary", "arbitrary"),
                                             vmem_limit_bytes=VMEM_LIMIT),
        name="hgrn2",
    )(n, w, hg_lb, hg_norm)


_PAIRS = M_HEADS // 2
_PAIR_W = 2 * M_HEADDIM


def _ssd_kernel(n_ref, wz_ref, wxbc_ref, wdt_ref, wdtt_ref, cw_ref, cb_ref, dtb_ref, aneg_ref,
                dtbt_ref, anegt_ref, dsk_ref, gn_ref, y_ref,
                st_ref, hist_ref, edge_ref, xbc_ref, zs_ref, yacc_ref):
    @pl.when(pl.program_id(1) == 0)
    def _():
        st_ref[...] = jnp.zeros_like(st_ref)
        hist_ref[...] = jnp.zeros_like(hist_ref)

    n = n_ref[...].astype(BF16)
    r = lax.broadcasted_iota(jnp.int32, (SEQ_TILE, SEQ_TILE), 0)
    c = lax.broadcasted_iota(jnp.int32, (SEQ_TILE, SEQ_TILE), 1)
    same_chunk = c // CHUNK == r // CHUNK
    tril_chunk = ((c <= r) & same_chunk).astype(BF16)
    triu_chunk = ((r <= c) & same_chunk).astype(BF16)
    e_r = lax.broadcasted_iota(jnp.int32, (_PAIR_W, COL_BLOCK), 0)
    e_c = lax.broadcasted_iota(jnp.int32, (_PAIR_W, COL_BLOCK), 1)

    dt_c = _softplus(_dot(n, wdt_ref[...]) + dtb_ref[...])
    a_c = dt_c * aneg_ref[...]
    acs_c = jnp.concatenate([_dot_sel(tril_chunk[0:CHUNK, 0:CHUNK], a_c[ci * CHUNK:(ci + 1) * CHUNK])
                             for ci in range(N_CHUNKS)], axis=0)
    dt_parts = jnp.concatenate(_split_bf16(dt_c, 2), axis=1)
    acs_parts = jnp.concatenate(_split_bf16(acs_c, 3), axis=1)
    a_t = _softplus(_dot_nt(wdtt_ref[...], n) + dtbt_ref[...]) * anegt_ref[...]
    acs_t = _dot_sel_r(a_t, triu_chunk, 3)

    first_expand = (M_CONV_DIM - M_DINNER) // COL_BLOCK
    dt_blocks, acs_blocks = [], []
    for cb_i in range(M_CONV_DIM // COL_BLOCK):
        cols = slice(cb_i * COL_BLOCK, (cb_i + 1) * COL_BLOCK)
        x = _dot(n, wxbc_ref[:, cols])
        cw = cw_ref[:, cols]
        cb = cb_ref[:, cols]
        x1 = pltpu.roll(x, 1, 0)
        near = x * cw[3:4] + x1 * cw[2:3]
        far = x * cw[1:2] + x1 * cw[0:1]
        xbc_ref[:, cols] = _silu(near + pltpu.roll(far, 2, 0) + cb)
        edge_ref[0:8, cols] = hist_ref[:, cols]
        edge_ref[8:16, cols] = x[0:8]
        top = _bcast_rows(cb, 8)
        for kk in range(M_CONV):
            top = top + edge_ref[8 - (M_CONV - 1) + kk:16 - (M_CONV - 1) + kk, cols] * cw[kk:kk + 1]
        xbc_ref[0:8, cols] = _silu(top)
        hist_ref[:, cols] = x[SEQ_TILE - 8:SEQ_TILE]
        if cb_i >= first_expand:
            expand = ((e_c + (cb_i - first_expand) * COL_BLOCK) // M_HEADDIM == e_r).astype(BF16)
            dt_blocks.append(_dot(dt_parts, jnp.concatenate([expand] * 2, axis=0)))
            acs_blocks.append(_dot(acs_parts, jnp.concatenate([expand] * 3, axis=0)))
    dt_e = jnp.concatenate(dt_blocks, axis=1)
    acs_e = jnp.concatenate(acs_blocks, axis=1)

    lane_left = lax.broadcasted_iota(jnp.int32, (_PAIRS, _PAIR_W), 1) < CHUNK
    acs_pair = []
    for vc in range(SEQ_TILE // _PAIR_W):
        ev = acs_t[0:_PAIRS, vc * _PAIR_W:(vc + 1) * _PAIR_W]
        od = acs_t[_PAIRS:2 * _PAIRS, vc * _PAIR_W:(vc + 1) * _PAIR_W]
        acs_pair.append(jnp.where(lane_left, ev, pltpu.roll(od, CHUNK, 1)))
        acs_pair.append(jnp.where(lane_left, pltpu.roll(ev, CHUNK, 1), od))

    xdt = xbc_ref[:, 0:M_DINNER] * dt_e
    w_start = jnp.exp(acs_e)

    t_i = lax.broadcasted_iota(jnp.int32, (CHUNK, _PAIR_W), 0)
    l_i = lax.broadcasted_iota(jnp.int32, (CHUNK, _PAIR_W), 1)
    causal2 = (l_i % CHUNK) <= t_i
    left = l_i < M_HEADDIM

    row_chunk = lax.broadcasted_iota(jnp.int32, (SEQ_TILE, M_STATE), 0) // CHUNK
    acs_last = [acs_e[(ci + 1) * CHUNK - 1:(ci + 1) * CHUNK] for ci in range(N_CHUNKS)]
    acs_last_rows = jnp.concatenate([_bcast_rows(a, CHUNK) for a in acs_last], axis=0)
    xw = (xdt * jnp.exp(acs_last_rows - acs_e)).astype(BF16)
    xdt_b = xdt.astype(BF16)

    def by_chunk(v):
        zero = jnp.zeros_like(v)
        return jnp.concatenate([jnp.where(row_chunk == ci, v, zero) for ci in range(N_CHUNKS)], axis=1)

    for g in range(M_GROUPS):
        gl = slice(g * M_GROUP_W, (g + 1) * M_GROUP_W)
        b_g = xbc_ref[:, M_DINNER + g * M_STATE:M_DINNER + (g + 1) * M_STATE].astype(BF16)
        c_g = xbc_ref[:, M_DINNER + (M_GROUPS + g) * M_STATE:
                      M_DINNER + (M_GROUPS + g + 1) * M_STATE].astype(BF16)
        zs_ref[:, gl] = _silu(_dot(n, wz_ref[:, gl]))
        b_dup = jnp.concatenate([b_g[ci * CHUNK:(ci + 1) * CHUNK] for ci in range(N_CHUNKS) for _ in (0, 1)],
                                axis=0)
        cb_all = _dot_nt(c_g, b_dup)
        for ci in range(N_CHUNKS):
            rows = slice(ci * CHUNK, (ci + 1) * CHUNK)
            cb2 = cb_all[rows, ci * _PAIR_W:(ci + 1) * _PAIR_W]
            for qq in range(M_GROUP_W // (2 * _PAIR_W)):
                ms, rhss = [], []
                for pp in (2 * qq, 2 * qq + 1):
                    p = g * (M_GROUP_W // _PAIR_W) + pp
                    pl_ = slice(p * _PAIR_W, (p + 1) * _PAIR_W)
                    seg = acs_e[rows, pl_] - acs_pair[ci][p:p + 1, :]
                    decay = jnp.exp(jnp.where(causal2, seg, -jnp.inf))
                    ms.append((cb2 * decay).astype(BF16))
                    xp = xdt_b[rows, pl_]
                    zero = jnp.zeros_like(xp)
                    rhss.append(jnp.concatenate([jnp.where(left, xp, zero), jnp.where(left, zero, xp)], axis=0))
                zblk = jnp.zeros_like(rhss[0])
                rhs = jnp.concatenate([jnp.concatenate([rhss[0], zblk], axis=1),
                                       jnp.concatenate([zblk, rhss[1]], axis=1)], axis=0)
                q0 = (g * (M_GROUP_W // _PAIR_W) + 2 * qq) * _PAIR_W
                yacc_ref[rows, q0:q0 + 2 * _PAIR_W] = _dot(jnp.concatenate(ms, axis=1), rhs)
        incr = _dot_tn(by_chunk(b_g), xw[:, gl])
        st = st_ref[g]
        starts = []
        for ci in range(N_CHUNKS):
            starts.append(st.astype(BF16))
            st = st * jnp.exp(acs_last[ci][:, gl]) + incr[ci * M_STATE:(ci + 1) * M_STATE]
        st_ref[g] = st
        y_inter = _dot(by_chunk(c_g), jnp.concatenate(starts, axis=0))
        y = (yacc_ref[:, gl] + y_inter * w_start[:, gl] + dsk_ref[:, gl] * xbc_ref[:, gl]) * zs_ref[:, gl]
        y_ref[:, gl] = _rms(y, gn_ref[:, gl]).astype(BF16)


def _ssd(n, wz, wxbc, wdt, wdtt, cw, cb, dtb, aneg, dtbt, anegt, dsk, gn, batch, seq):
    tiles = seq // SEQ_TILE
    seq_spec = pl.BlockSpec((SEQ_TILE, D_MODEL), lambda b, j: (b * tiles + j, 0))
    params = (wz, wxbc, wdt, wdtt, cw, cb, dtb, aneg, dtbt, anegt, dsk, gn)
    return pl.pallas_call(
        _ssd_kernel,
        grid=(batch, tiles),
        in_specs=[seq_spec] + [_resident(a.shape) for a in params],
        out_specs=pl.BlockSpec((SEQ_TILE, M_DINNER), lambda b, j: (b * tiles + j, 0)),
        out_shape=jax.ShapeDtypeStruct((batch * seq, M_DINNER), BF16),
        scratch_shapes=[
            pltpu.VMEM((M_GROUPS, M_STATE, M_GROUP_W), F32),
            pltpu.VMEM((8, M_CONV_DIM), F32),
            pltpu.VMEM((16, M_CONV_DIM), F32),
            pltpu.VMEM((SEQ_TILE, M_CONV_DIM), F32),
            pltpu.VMEM((SEQ_TILE, M_DINNER), F32),
            pltpu.VMEM((SEQ_TILE, M_DINNER), F32),
        ],
        compiler_params=pltpu.CompilerParams(dimension_semantics=("arbitrary", "arbitrary"),
                                             vmem_limit_bytes=VMEM_LIMIT),
        name="ssd",
    )(n, *params)


def _merge_kernel(h_ref, n_ref, ohg_ref, y_ref, wbr_ref, whg_ref, wssm_ref, wout_ref, o_ref):
    br = _sigmoid(_dot(n_ref[...].astype(BF16), wbr_ref[...]))
    mixed = (br[:, 0:D_MODEL] * _dot(ohg_ref[...], whg_ref[...])
             + br[:, D_MODEL:2 * D_MODEL] * _dot(y_ref[...], wssm_ref[...]))
    o_ref[...] = h_ref[...] + _dot(mixed.astype(BF16), wout_ref[...])


def _merge(h, n, ohg, y, wbr, whg, wssm, wout):
    t = h.shape[0]
    return pl.pallas_call(
        _merge_kernel,
        grid=(t // ROW_TILE,),
        in_specs=[_row_spec(D_MODEL), _row_spec(D_MODEL), _row_spec(HG_VDIM), _row_spec(M_DINNER),
                  _resident(wbr.shape), _resident(whg.shape), _resident(wssm.shape),
                  _resident(wout.shape)],
        out_specs=_row_spec(D_MODEL),
        out_shape=jax.ShapeDtypeStruct((t, D_MODEL), F32),
        compiler_params=pltpu.CompilerParams(dimension_semantics=("arbitrary",),
                                             vmem_limit_bytes=VMEM_LIMIT),
        name="merge",
    )(h, n, ohg, y, wbr, whg, wssm, wout)


def kernel(x, p, ffn1_norm, ffn1_w13, ffn1_w2, mix_norm, w_in, conv_w, conv_b, dt_bias, a_log,
           d_skip, ssm_norm, hg_lb, hg_norm, w_hg_out, w_ssm_out, w_out, ffn2_norm, ffn2_w13,
           ffn2_w2, ple_norm, w_ple_gate, w_ple_proj, final_norm):
    batch, seq, _ = x.shape
    assert ffn1_w13.shape[0] == 1 and hg_lb.shape[0] == 2, "single-layer block"
    assert seq % SEQ_TILE == 0 and (batch * seq) % ROW_TILE == 0
    t = batch * seq
    row = lambda v: v.reshape(1, -1).astype(F32)
    col = lambda v: v.reshape(-1, 1).astype(F32)
    expand = lambda v: jnp.repeat(v, M_HEADDIM, axis=-1)
    pad_heads = lambda v: jnp.pad(v, [(0, 0)] * (v.ndim - 1) + [(0, _PAIR_W - M_HEADS)])
    even_odd = lambda v: jnp.concatenate([v[..., 0::2], v[..., 1::2]], axis=-1)

    sizes = (HG_KDIM, HG_KDIM, HG_VDIM, HG_VDIM, M_DINNER, M_CONV_DIM, M_HEADS, 2 * D_MODEL)
    offs = [0]
    for s in sizes:
        offs.append(offs[-1] + s)
    w = w_in[0].astype(BF16)
    w_dt = w[:, offs[6]:offs[7]]
    w_br = w[:, offs[7]:offs[8]]
    a_neg = -jnp.exp(a_log[0].astype(F32))
    dtb = dt_bias[0].astype(F32)

    x2 = x.reshape(t, D_MODEL)
    h1, n = _ffn1(x2, row(ffn1_norm[0]), ffn1_w13[0].astype(BF16), ffn1_w2[0].astype(BF16),
                  row(mix_norm[0]))
    o_hg = _hgrn(n, w, hg_lb.astype(F32), row(hg_norm[0]), batch, seq)
    y = _ssd(n, w[:, offs[4]:offs[5]], w[:, offs[5]:offs[6]], pad_heads(w_dt), even_odd(w_dt).T,
             conv_w[0].astype(F32), row(conv_b[0]), row(pad_heads(dtb)), row(pad_heads(a_neg)),
             col(even_odd(dtb)), col(even_odd(a_neg)),
             row(expand(d_skip[0].astype(F32))), row(ssm_norm[0]), batch, seq)
    h2 = _merge(h1, n, o_hg, y, w_br, w_hg_out[0].astype(BF16), w_ssm_out[0].astype(BF16),
                w_out[0].astype(BF16))
    out = _ffn2(h2, row(ffn2_norm[0]), ffn2_w13[0].astype(BF16), ffn2_w2[0].astype(BF16),
                p[0].reshape(t, PLE_DIM), row(ple_norm[0]), w_ple_gate[0].astype(BF16),
                w_ple_proj[0].astype(BF16), row(final_norm))
    return out.reshape(batch, seq, D_MODEL)
```

```python
import jax
import jax.numpy as jnp
from jax import lax
from jax.experimental import pallas as pl
from jax.experimental.pallas import tpu as pltpu

F32 = jnp.float32
BF16 = jnp.bfloat16

D_MODEL = 1024
D_FF = 2816
PLE_DIM = 256
EPS = 1e-6
HG_HEADS = 8
HG_DK = 128
HG_DV = 128
HG_KDIM = HG_HEADS * HG_DK
HG_VDIM = HG_HEADS * HG_DV
HG_BLOCK = 16
M_DINNER = 2048
M_HEADDIM = 64
M_HEADS = 32
M_STATE = 128
M_GROUPS = 4
M_CONV = 4
M_GROUP_W = M_DINNER // M_GROUPS
M_CONV_DIM = M_DINNER + 2 * M_GROUPS * M_STATE
CHUNK = 64

ROW_TILE = 512
SEQ_TILE = 256
N_CHUNKS = SEQ_TILE // CHUNK
COL_BLOCK = 512
VMEM_LIMIT = 56 * 1024 * 1024


def _dot(a, b):
    return jnp.dot(a, b, preferred_element_type=F32)


def _dot_nt(a, b):
    return lax.dot_general(a, b, (((1,), (1,)), ((), ())), preferred_element_type=F32)


def _dot_tn(a, b):
    return lax.dot_general(a, b, (((0,), (0,)), ((), ())), preferred_element_type=F32)


def _split_bf16(x, parts):
    out, rem = [], x
    for _ in range(parts):
        p = rem.astype(BF16)
        out.append(p)
        rem = rem - p.astype(F32)
    return out


def _dot_sel(sel, x, parts=3):
    return _dot(jnp.concatenate([sel] * parts, axis=1), jnp.concatenate(_split_bf16(x, parts), axis=0))


def _dot_sel_r(x, sel, parts=3):
    return _dot(jnp.concatenate(_split_bf16(x, parts), axis=1), jnp.concatenate([sel] * parts, axis=0))


def _rms(x, g):
    ms = jnp.mean(x * x, axis=-1, keepdims=True)
    return x * lax.rsqrt(ms + EPS) * g


def _sigmoid(x):
    return 0.5 * jnp.tanh(0.5 * x) + 0.5


def _silu(x):
    h = 0.5 * x
    return h + h * jnp.tanh(h)


def _softplus(x):
    return jnp.maximum(x, 0.0) + jnp.log(1.0 + jnp.exp(-jnp.abs(x)))


def _bcast_rows(row, rows):
    return jnp.broadcast_to(row, (rows, row.shape[-1]))


def _resident(shape):
    nd = len(shape)
    return pl.BlockSpec(shape, lambda *_: (0,) * nd, pipeline_mode=pl.Buffered(1))


def _resident_cols(rows, width, block):
    return pl.BlockSpec((rows, width), lambda *_: (0, block), pipeline_mode=pl.Buffered(1))


def _swiglu_residual(x, g_ref, w13_ref, w2_ref):
    n = _rms(x, g_ref[...]).astype(BF16)
    gate = _dot(n, w13_ref[:, 0:D_FF])
    up = _dot(n, w13_ref[:, D_FF:2 * D_FF])
    act = (_silu(gate) * up).astype(BF16)
    return x + 0.5 * _dot(act, w2_ref[...])


def _ffn1_kernel(x_ref, g_ref, w13_ref, w2_ref, mixg_ref, h_ref, n_ref):
    h = _swiglu_residual(x_ref[...], g_ref, w13_ref, w2_ref)
    h_ref[...] = h
    n_ref[...] = _rms(h, mixg_ref[...])


def _ffn2_kernel(x_ref, g_ref, w13_ref, w2_ref, p_ref, pleg_ref, wpg_ref, wpp_ref, fing_ref, o_ref):
    h = _swiglu_residual(x_ref[...], g_ref, w13_ref, w2_ref)
    gate = _sigmoid(_dot(_rms(h, pleg_ref[...]).astype(BF16), wpg_ref[...]))
    emb = _dot(p_ref[...].astype(BF16), wpp_ref[...])
    h = h + gate * emb
    o_ref[...] = _rms(h, fing_ref[...])


def _row_spec(width):
    return pl.BlockSpec((ROW_TILE, width), lambda i: (i, 0))


def _ffn1(x, g, w13, w2, mixg):
    t = x.shape[0]
    return pl.pallas_call(
        _ffn1_kernel,
        grid=(t // ROW_TILE,),
        in_specs=[_row_spec(D_MODEL), _resident((1, D_MODEL)), _resident(w13.shape),
                  _resident(w2.shape), _resident((1, D_MODEL))],
        out_specs=[_row_spec(D_MODEL), _row_spec(D_MODEL)],
        out_shape=[jax.ShapeDtypeStruct((t, D_MODEL), F32),
                   jax.ShapeDtypeStruct((t, D_MODEL), F32)],
        compiler_params=pltpu.CompilerParams(dimension_semantics=("arbitrary",),
                                             vmem_limit_bytes=VMEM_LIMIT),
        name="ffn1",
    )(x, g, w13, w2, mixg)


def _ffn2(x, g, w13, w2, p, pleg, wpg, wpp, fing):
    t = x.shape[0]
    return pl.pallas_call(
        _ffn2_kernel,
        grid=(t // ROW_TILE,),
        in_specs=[_row_spec(D_MODEL), _resident((1, D_MODEL)), _resident(w13.shape),
                  _resident(w2.shape), _row_spec(PLE_DIM), _resident((1, D_MODEL)),
                  _resident(wpg.shape), _resident(wpp.shape), _resident((1, D_MODEL))],
        out_specs=_row_spec(D_MODEL),
        out_shape=jax.ShapeDtypeStruct((t, D_MODEL), F32),
        compiler_params=pltpu.CompilerParams(dimension_semantics=("arbitrary",),
                                             vmem_limit_bytes=VMEM_LIMIT),
        name="ffn2",
    )(x, g, w13, w2, p, pleg, wpg, wpp, fing)


def _hgrn_kernel(n_ref, w_ref, lb_ref, gn_ref, o_ref,
                 st_ref, q_ref, k_ref, gl_ref, v_ref, gate_ref, oacc_ref):
    @pl.when(pl.program_id(1) == 0)
    def _():
        st_ref[...] = jnp.zeros_like(st_ref)

    n = n_ref[...].astype(BF16)
    a = lb_ref[...]
    e = jnp.exp(a - jnp.max(a, axis=0, keepdims=True))
    lb = e[0:1] / jnp.sum(e, axis=0, keepdims=True)

    r = lax.broadcasted_iota(jnp.int32, (CHUNK, CHUNK), 0)
    c = lax.broadcasted_iota(jnp.int32, (CHUNK, CHUNK), 1)
    tril_blk = ((c <= r) & (c // HG_BLOCK == r // HG_BLOCK)).astype(BF16)
    f_raws = [_dot(n, w_ref[:, HG_KDIM + cb * COL_BLOCK:HG_KDIM + (cb + 1) * COL_BLOCK])
              for cb in range(HG_KDIM // COL_BLOCK)]
    q_ref[...] = _dot(n, w_ref[:, 0:HG_KDIM]) * (HG_DK ** -0.5)
    v_ref[...] = _dot(n, w_ref[:, 2 * HG_KDIM:2 * HG_KDIM + HG_VDIM]).astype(BF16)
    for cb, f_raw in enumerate(f_raws):
        cols = slice(cb * COL_BLOCK, (cb + 1) * COL_BLOCK)
        f = lb[:, cols] + (1.0 - lb[:, cols]) * _sigmoid(f_raw)
        k_ref[:, cols] = 1.0 - f
        logf = jnp.log(f)
        for ci in range(N_CHUNKS):
            rows = slice(ci * CHUNK, (ci + 1) * CHUNK)
            gl_ref[rows, cols] = _dot_sel(tril_blk, logf[rows])

    t_i = lax.broadcasted_iota(jnp.int32, (CHUNK, 4 * CHUNK), 0)
    c_i = lax.broadcasted_iota(jnp.int32, (CHUNK, 4 * CHUNK), 1)
    s_i, grp = c_i % CHUNK, c_i // CHUNK
    bt, bs = t_i // HG_BLOCK, s_i // HG_BLOCK
    m_all = (((grp == 0) & (bt == bs) & (s_i <= t_i))
             | ((grp == 1) & (bt % 2 == 1) & (bs == bt - 1))
             | ((grp == 2) & (bt >= 2) & (bs < 2)))
    zero_blk = jnp.zeros((CHUNK, HG_DK), BF16)
    ones_blk = jnp.ones((HG_BLOCK, HG_KDIM), F32)
    gate_cols = HG_VDIM // N_CHUNKS

    atts = {}
    prepped = []
    for ci in range(N_CHUNKS):
        rows = slice(ci * CHUNK, (ci + 1) * CHUNK)
        q, k, gl, v = q_ref[rows, :], k_ref[rows, :], gl_ref[rows, :], v_ref[rows, :]
        tot = [gl[HG_BLOCK * i + HG_BLOCK - 1:HG_BLOCK * (i + 1), :] for i in range(4)]
        e_t = [jnp.exp(t) for t in tot]
        e01, e23 = e_t[0] * e_t[1], e_t[2] * e_t[3]
        dec = e01 * e23
        tot_rows = jnp.concatenate([_bcast_rows(t, HG_BLOCK) for t in tot], axis=0)
        qd = q * jnp.exp(gl)
        ki = k * jnp.exp(-gl)
        ke = k * jnp.exp(tot_rows - gl)
        qd32 = qd * jnp.concatenate(
            [ones_blk, _bcast_rows(e_t[0], HG_BLOCK), ones_blk, _bcast_rows(e_t[2], HG_BLOCK)], axis=0)
        ke32 = ke * jnp.concatenate(
            [_bcast_rows(e_t[1], HG_BLOCK), ones_blk, _bcast_rows(e_t[3], HG_BLOCK), ones_blk], axis=0)
        qd64 = qd32 * jnp.concatenate(
            [ones_blk, ones_blk, _bcast_rows(e01, HG_BLOCK), _bcast_rows(e01, HG_BLOCK)], axis=0)
        ke64 = ke32 * jnp.concatenate(
            [_bcast_rows(e23, HG_BLOCK), _bcast_rows(e23, HG_BLOCK), ones_blk, ones_blk], axis=0)
        qd, ki, ke, qd32, ke32, qd64, ke64 = (
            t.astype(BF16) for t in (qd, ki, ke, qd32, ke32, qd64, ke64))
        prepped.append((v, dec, qd64, ke64))
        g0 = 2 * HG_KDIM + HG_VDIM + ci * gate_cols
        gate_ref[:, ci * gate_cols:(ci + 1) * gate_cols] = _silu(_dot(n, w_ref[:, g0:g0 + gate_cols]))
        for h in range(HG_HEADS):
            kl = slice(h * HG_DK, (h + 1) * HG_DK)
            lhs = jnp.concatenate([qd[:, kl], qd32[:, kl]], axis=1)
            rhs_t = jnp.concatenate([jnp.concatenate([ki[:, kl], zero_blk], axis=1),
                                     jnp.concatenate([ke[:, kl], zero_blk], axis=1),
                                     jnp.concatenate([zero_blk, ke32[:, kl]], axis=1),
                                     jnp.concatenate([zero_blk, zero_blk], axis=1)], axis=0)
            atts[ci, h] = jnp.where(m_all, _dot_nt(lhs, rhs_t), 0.0).astype(BF16)
    gn = gn_ref[...]
    first_head = lax.broadcasted_iota(jnp.int32, (CHUNK, 2 * HG_DV), 1) < HG_DV
    for ci in range(N_CHUNKS):
        rows = slice(ci * CHUNK, (ci + 1) * CHUNK)
        v, dec, qd64, ke64 = prepped[ci]
        for h in range(0, HG_HEADS, 2):
            kl2 = slice(h * HG_DK, (h + 2) * HG_DK)
            vl2 = slice(h * HG_DV, (h + 2) * HG_DV)
            st_a, st_b = st_ref[h], st_ref[h + 1]
            v2 = v[:, vl2]
            zero_v = jnp.zeros_like(v2)
            v_a = jnp.where(first_head, v2, zero_v)
            v_b = jnp.where(first_head, zero_v, v2)
            v_bd = jnp.concatenate([v_a, v_a, v_a, zero_v, v_b, v_b, v_b, zero_v], axis=0)
            zero_s = jnp.zeros((HG_DV, HG_DK), BF16)
            st_bd = jnp.concatenate([jnp.concatenate([st_a.astype(BF16), zero_s], axis=1),
                                     jnp.concatenate([zero_s, st_b.astype(BF16)], axis=1)], axis=0)
            att2 = jnp.concatenate([atts[ci, h], atts[ci, h + 1]], axis=1)
            oacc_ref[rows, vl2] = _dot(att2, v_bd) + _dot_nt(qd64[:, kl2], st_bd)
            incr = _dot_tn(v2, ke64[:, kl2])
            st_ref[h] = st_a * dec[:, h * HG_DK:(h + 1) * HG_DK] + incr[0:HG_DV, 0:HG_DK]
            st_ref[h + 1] = (st_b * dec[:, (h + 1) * HG_DK:(h + 2) * HG_DK]
                             + incr[HG_DV:2 * HG_DV, HG_DK:2 * HG_DK])
        for h in range(HG_HEADS):
            vl = slice(h * HG_DV, (h + 1) * HG_DV)
            o_ref[rows, vl] = (_rms(oacc_ref[rows, vl], gn[:, vl]) * gate_ref[rows, vl]).astype(BF16)


def _hgrn(n, w, hg_lb, hg_norm, batch, seq):
    tiles = seq // SEQ_TILE
    seq_spec = pl.BlockSpec((SEQ_TILE, D_MODEL), lambda b, j: (b * tiles + j, 0))
    return pl.pallas_call(
        _hgrn_kernel,
        grid=(batch, tiles),
        in_specs=[seq_spec, _resident_cols(D_MODEL, 2 * HG_KDIM + 2 * HG_VDIM, 0), _resident(hg_lb.shape),
                  _resident((1, HG_VDIM))],
        out_specs=pl.BlockSpec((SEQ_TILE, HG_VDIM), lambda b, j: (b * tiles + j, 0)),
        out_shape=jax.ShapeDtypeStruct((batch * seq, HG_VDIM), BF16),
        scratch_shapes=[
            pltpu.VMEM((HG_HEADS, HG_DV, HG_DK), F32),
            pltpu.VMEM((SEQ_TILE, HG_KDIM), F32),
            pltpu.VMEM((SEQ_TILE, HG_KDIM), F32),
            pltpu.VMEM((SEQ_TILE, HG_KDIM), F32),
            pltpu.VMEM((SEQ_TILE, HG_VDIM), BF16),
            pltpu.VMEM((SEQ_TILE, HG_VDIM), F32),
            pltpu.VMEM((SEQ_TILE, HG_VDIM), F32),
        ],
        compiler_params=pltpu.CompilerParams(dimension_semantics=("arbitrary", "arbitrary"),
                                             vmem_limit_bytes=VMEM_LIMIT),
        name="hgrn2",
    )(n, w, hg_lb, hg_norm)


_PAIRS = M_HEADS // 2
_PAIR_W = 2 * M_HEADDIM


def _ssd_kernel(n_ref, wz_ref, wxbc_ref, wdt_ref, wdtt_ref, cw_ref, cb_ref, dtb_ref, aneg_ref,
                dtbt_ref, anegt_ref, dsk_ref, gn_ref, y_ref,
                st_ref, hist_ref, edge_ref, xbc_ref, zs_ref, yacc_ref):
    @pl.when(pl.program_id(1) == 0)
    def _():
        st_ref[...] = jnp.zeros_like(st_ref)
        hist_ref[...] = jnp.zeros_like(hist_ref)

    n = n_ref[...].astype(BF16)
    r = lax.broadcasted_iota(jnp.int32, (SEQ_TILE, SEQ_TILE), 0)
    c = lax.broadcasted_iota(jnp.int32, (SEQ_TILE, SEQ_TILE), 1)
    same_chunk = c // CHUNK == r // CHUNK
    tril_chunk = ((c <= r) & same_chunk).astype(BF16)
    triu_chunk = ((r <= c) & same_chunk).astype(BF16)
    e_r = lax.broadcasted_iota(jnp.int32, (_PAIR_W, COL_BLOCK), 0)
    e_c = lax.broadcasted_iota(jnp.int32, (_PAIR_W, COL_BLOCK), 1)

    dt_c = _softplus(_dot(n, wdt_ref[...]) + dtb_ref[...])
    a_c = dt_c * aneg_ref[...]
    acs_c = jnp.concatenate([_dot_sel(tril_chunk[0:CHUNK, 0:CHUNK], a_c[ci * CHUNK:(ci + 1) * CHUNK])
                             for ci in range(N_CHUNKS)], axis=0)
    dt_parts = jnp.concatenate(_split_bf16(dt_c, 2), axis=1)
    acs_parts = jnp.concatenate(_split_bf16(acs_c, 3), axis=1)
    a_t = _softplus(_dot_nt(wdtt_ref[...], n) + dtbt_ref[...]) * anegt_ref[...]
    acs_t = _dot_sel_r(a_t, triu_chunk, 3)

    first_expand = (M_CONV_DIM - M_DINNER) // COL_BLOCK
    dt_blocks, acs_blocks = [], []
    for cb_i in range(M_CONV_DIM // COL_BLOCK):
        cols = slice(cb_i * COL_BLOCK, (cb_i + 1) * COL_BLOCK)
        x = _dot(n, wxbc_ref[:, cols])
        cw = cw_ref[:, cols]
        cb = cb_ref[:, cols]
        x1 = pltpu.roll(x, 1, 0)
        near = x * cw[3:4] + x1 * cw[2:3]
        far = x * cw[1:2] + x1 * cw[0:1]
        xbc_ref[:, cols] = _silu(near + pltpu.roll(far, 2, 0) + cb)
        edge_ref[0:8, cols] = hist_ref[:, cols]
        edge_ref[8:16, cols] = x[0:8]
        top = _bcast_rows(cb, 8)
        for kk in range(M_CONV):
            top = top + edge_ref[8 - (M_CONV - 1) + kk:16 - (M_CONV - 1) + kk, cols] * cw[kk:kk + 1]
        xbc_ref[0:8, cols] = _silu(top)
        hist_ref[:, cols] = x[SEQ_TILE - 8:SEQ_TILE]
        if cb_i >= first_expand:
            expand = ((e_c + (cb_i - first_expand) * COL_BLOCK) // M_HEADDIM == e_r).astype(BF16)
            dt_blocks.append(_dot(dt_parts, jnp.concatenate([expand] * 2, axis=0)))
            acs_blocks.append(_dot(acs_parts, jnp.concatenate([expand] * 3, axis=0)))
    dt_e = jnp.concatenate(dt_blocks, axis=1)
    acs_e = jnp.concatenate(acs_blocks, axis=1)

    lane_left = lax.broadcasted_iota(jnp.int32, (_PAIRS, _PAIR_W), 1) < CHUNK
    acs_pair = []
    for vc in range(SEQ_TILE // _PAIR_W):
        ev = acs_t[0:_PAIRS, vc * _PAIR_W:(vc + 1) * _PAIR_W]
        od = acs_t[_PAIRS:2 * _PAIRS, vc * _PAIR_W:(vc + 1) * _PAIR_W]
        acs_pair.append(jnp.where(lane_left, ev, pltpu.roll(od, CHUNK, 1)))
        acs_pair.append(jnp.where(lane_left, pltpu.roll(ev, CHUNK, 1), od))

    xdt = xbc_ref[:, 0:M_DINNER] * dt_e
    w_start = jnp.exp(acs_e)

    t_i = lax.broadcasted_iota(jnp.int32, (CHUNK, _PAIR_W), 0)
    l_i = lax.broadcasted_iota(jnp.int32, (CHUNK, _PAIR_W), 1)
    causal2 = (l_i % CHUNK) <= t_i
    left = l_i < M_HEADDIM

    row_chunk = lax.broadcasted_iota(jnp.int32, (SEQ_TILE, M_STATE), 0) // CHUNK
    acs_last = [acs_e[(ci + 1) * CHUNK - 1:(ci + 1) * CHUNK] for ci in range(N_CHUNKS)]
    acs_last_rows = jnp.concatenate([_bcast_rows(a, CHUNK) for a in acs_last], axis=0)
    xw = (xdt * jnp.exp(acs_last_rows - acs_e)).astype(BF16)
    xdt_b = xdt.astype(BF16)

    def by_chunk(v):
        zero = jnp.zeros_like(v)
        return jnp.concatenate([jnp.where(row_chunk == ci, v, zero) for ci in range(N_CHUNKS)], axis=1)

    for g in range(M_GROUPS):
        gl = slice(g * M_GROUP_W, (g + 1) * M_GROUP_W)
        b_g = xbc_ref[:, M_DINNER + g * M_STATE:M_DINNER + (g + 1) * M_STATE].astype(BF16)
        c_g = xbc_ref[:, M_DINNER + (M_GROUPS + g) * M_STATE:
                      M_DINNER + (M_GROUPS + g + 1) * M_STATE].astype(BF16)
        zs_ref[:, gl] = _silu(_dot(n, wz_ref[:, gl]))
        b_dup = jnp.concatenate([b_g[ci * CHUNK:(ci + 1) * CHUNK] for ci in range(N_CHUNKS) for _ in (0, 1)],
                                axis=0)
        cb_all = _dot_nt(c_g, b_dup)
        for ci in range(N_CHUNKS):
            rows = slice(ci * CHUNK, (ci + 1) * CHUNK)
            cb2 = cb_all[rows, ci * _PAIR_W:(ci + 1) * _PAIR_W]
            for pp in range(M_GROUP_W // _PAIR_W):
                p = g * (M_GROUP_W // _PAIR_W) + pp
                pl_ = slice(p * _PAIR_W, (p + 1) * _PAIR_W)
                seg = acs_e[rows, pl_] - acs_pair[ci][p:p + 1, :]
                decay = jnp.exp(jnp.where(causal2, seg, -jnp.inf))
                m = (cb2 * decay).astype(BF16)
                xp = xdt_b[rows, pl_]
                zero = jnp.zeros_like(xp)
                rhs = jnp.concatenate([jnp.where(left, xp, zero), jnp.where(left, zero, xp)], axis=0)
                yacc_ref[rows, pl_] = _dot(m, rhs)
        incr = _dot_tn(by_chunk(b_g), xw[:, gl])
        st = st_ref[g]
        starts = []
        for ci in range(N_CHUNKS):
            starts.append(st.astype(BF16))
            st = st * jnp.exp(acs_last[ci][:, gl]) + incr[ci * M_STATE:(ci + 1) * M_STATE]
        st_ref[g] = st
        y_inter = _dot(by_chunk(c_g), jnp.concatenate(starts, axis=0))
        y = (yacc_ref[:, gl] + y_inter * w_start[:, gl] + dsk_ref[:, gl] * xbc_ref[:, gl]) * zs_ref[:, gl]
        y_ref[:, gl] = _rms(y, gn_ref[:, gl]).astype(BF16)


def _ssd(n, w_all, wdt, wdtt, cw, cb, dtb, aneg, dtbt, anegt, dsk, gn, batch, seq):
    tiles = seq // SEQ_TILE
    seq_spec = pl.BlockSpec((SEQ_TILE, D_MODEL), lambda b, j: (b * tiles + j, 0))
    params = (wdt, wdtt, cw, cb, dtb, aneg, dtbt, anegt, dsk, gn)
    z_off, xbc_off = 2 * HG_KDIM + 2 * HG_VDIM, 2 * HG_KDIM + 2 * HG_VDIM + M_DINNER
    assert z_off % M_DINNER == 0 and xbc_off % M_CONV_DIM == 0
    return pl.pallas_call(
        _ssd_kernel,
        grid=(batch, tiles),
        in_specs=[seq_spec, _resident_cols(D_MODEL, M_DINNER, z_off // M_DINNER),
                  _resident_cols(D_MODEL, M_CONV_DIM, xbc_off // M_CONV_DIM)]
                 + [_resident(a.shape) for a in params],
        out_specs=pl.BlockSpec((SEQ_TILE, M_DINNER), lambda b, j: (b * tiles + j, 0)),
        out_shape=jax.ShapeDtypeStruct((batch * seq, M_DINNER), BF16),
        scratch_shapes=[
            pltpu.VMEM((M_GROUPS, M_STATE, M_GROUP_W), F32),
            pltpu.VMEM((8, M_CONV_DIM), F32),
            pltpu.VMEM((16, M_CONV_DIM), F32),
            pltpu.VMEM((SEQ_TILE, M_CONV_DIM), F32),
            pltpu.VMEM((SEQ_TILE, M_DINNER), F32),
            pltpu.VMEM((SEQ_TILE, M_DINNER), F32),
        ],
        compiler_params=pltpu.CompilerParams(dimension_semantics=("arbitrary", "arbitrary"),
                                             vmem_limit_bytes=VMEM_LIMIT),
        name="ssd",
    )(n, w_all, w_all, *params)


def _merge_kernel(h_ref, n_ref, ohg_ref, y_ref, wbr_ref, whg_ref, wssm_ref, wout_ref, o_ref):
    br = _sigmoid(_dot(n_ref[...].astype(BF16), wbr_ref[...]))
    mixed = (br[:, 0:D_MODEL] * _dot(ohg_ref[...], whg_ref[...])
             + br[:, D_MODEL:2 * D_MODEL] * _dot(y_ref[...], wssm_ref[...]))
    o_ref[...] = h_ref[...] + _dot(mixed.astype(BF16), wout_ref[...])


def _merge(h, n, ohg, y, wbr, whg, wssm, wout):
    t = h.shape[0]
    return pl.pallas_call(
        _merge_kernel,
        grid=(t // ROW_TILE,),
        in_specs=[_row_spec(D_MODEL), _row_spec(D_MODEL), _row_spec(HG_VDIM), _row_spec(M_DINNER),
                  _resident(wbr.shape), _resident(whg.shape), _resident(wssm.shape),
                  _resident(wout.shape)],
        out_specs=_row_spec(D_MODEL),
        out_shape=jax.ShapeDtypeStruct((t, D_MODEL), F32),
        compiler_params=pltpu.CompilerParams(dimension_semantics=("arbitrary",),
                                             vmem_limit_bytes=VMEM_LIMIT),
        name="merge",
    )(h, n, ohg, y, wbr, whg, wssm, wout)


def kernel(x, p, ffn1_norm, ffn1_w13, ffn1_w2, mix_norm, w_in, conv_w, conv_b, dt_bias, a_log,
           d_skip, ssm_norm, hg_lb, hg_norm, w_hg_out, w_ssm_out, w_out, ffn2_norm, ffn2_w13,
           ffn2_w2, ple_norm, w_ple_gate, w_ple_proj, final_norm):
    batch, seq, _ = x.shape
    assert ffn1_w13.shape[0] == 1 and hg_lb.shape[0] == 2, "single-layer block"
    assert seq % SEQ_TILE == 0 and (batch * seq) % ROW_TILE == 0
    t = batch * seq
    row = lambda v: v.reshape(1, -1).astype(F32)
    col = lambda v: v.reshape(-1, 1).astype(F32)
    expand = lambda v: jnp.repeat(v, M_HEADDIM, axis=-1)
    pad_heads = lambda v: jnp.pad(v, [(0, 0)] * (v.ndim - 1) + [(0, _PAIR_W - M_HEADS)])
    even_odd = lambda v: jnp.concatenate([v[..., 0::2], v[..., 1::2]], axis=-1)

    sizes = (HG_KDIM, HG_KDIM, HG_VDIM, HG_VDIM, M_DINNER, M_CONV_DIM, M_HEADS, 2 * D_MODEL)
    offs = [0]
    for s in sizes:
        offs.append(offs[-1] + s)
    w = w_in[0].astype(BF16)
    w_dt = w[:, offs[6]:offs[7]]
    w_br = w[:, offs[7]:offs[8]]
    a_neg = -jnp.exp(a_log[0].astype(F32))
    dtb = dt_bias[0].astype(F32)

    x2 = x.reshape(t, D_MODEL)
    h1, n = _ffn1(x2, row(ffn1_norm[0]), ffn1_w13[0].astype(BF16), ffn1_w2[0].astype(BF16),
                  row(mix_norm[0]))
    o_hg = _hgrn(n, w, hg_lb.astype(F32), row(hg_norm[0]), batch, seq)
    y = _ssd(n, w, pad_heads(w_dt), even_odd(w_dt).T,
             conv_w[0].astype(F32), row(conv_b[0]), row(pad_heads(dtb)), row(pad_heads(a_neg)),
             col(even_odd(dtb)), col(even_odd(a_neg)),
             row(expand(d_skip[0].astype(F32))), row(ssm_norm[0]), batch, seq)
    h2 = _merge(h1, n, o_hg, y, w_br, w_hg_out[0].astype(BF16), w_ssm_out[0].astype(BF16),
                w_out[0].astype(BF16))
    out = _ffn2(h2, row(ffn2_norm[0]), ffn2_w13[0].astype(BF16), ffn2_w2[0].astype(BF16),
                p[0].reshape(t, PLE_DIM), row(ple_norm[0]), w_ple_gate[0].astype(BF16),
                w_ple_proj[0].astype(BF16), row(final_norm))
    return out.reshape(batch, seq, D_MODEL)
```

```python
import jax
import jax.numpy as jnp
from jax import lax
from jax.experimental import pallas as pl
from jax.experimental.pallas import tpu as pltpu

F32 = jnp.float32
BF16 = jnp.bfloat16

D_MODEL = 1024
D_FF = 2816
PLE_DIM = 256
EPS = 1e-6
HG_HEADS = 8
HG_DK = 128
HG_DV = 128
HG_KDIM = HG_HEADS * HG_DK
HG_VDIM = HG_HEADS * HG_DV
HG_BLOCK = 16
M_DINNER = 2048
M_HEADDIM = 64
M_HEADS = 32
M_STATE = 128
M_GROUPS = 4
M_CONV = 4
M_GROUP_W = M_DINNER // M_GROUPS
M_CONV_DIM = M_DINNER + 2 * M_GROUPS * M_STATE
CHUNK = 64

ROW_TILE = 512
SEQ_TILE = 256
N_CHUNKS = SEQ_TILE // CHUNK
COL_BLOCK = 512
VMEM_LIMIT = 56 * 1024 * 1024


def _dot(a, b):
    return jnp.dot(a, b, preferred_element_type=F32)


def _dot_nt(a, b):
    return lax.dot_general(a, b, (((1,), (1,)), ((), ())), preferred_element_type=F32)


def _dot_tn(a, b):
    return lax.dot_general(a, b, (((0,), (0,)), ((), ())), preferred_element_type=F32)


def _split_bf16(x, parts):
    out, rem = [], x
    for _ in range(parts):
        p = rem.astype(BF16)
        out.append(p)
        rem = rem - p.astype(F32)
    return out


def _dot_sel(sel, x, parts=3):
    return _dot(jnp.concatenate([sel] * parts, axis=1), jnp.concatenate(_split_bf16(x, parts), axis=0))


def _dot_sel_r(x, sel, parts=3):
    return _dot(jnp.concatenate(_split_bf16(x, parts), axis=1), jnp.concatenate([sel] * parts, axis=0))


def _rms(x, g):
    ms = jnp.mean(x * x, axis=-1, keepdims=True)
    return x * lax.rsqrt(ms + EPS) * g


def _sigmoid(x):
    return 0.5 * jnp.tanh(0.5 * x) + 0.5


def _silu(x):
    h = 0.5 * x
    return h + h * jnp.tanh(h)


def _softplus(x):
    return jnp.maximum(x, 0.0) + jnp.log(1.0 + jnp.exp(-jnp.abs(x)))


def _bcast_rows(row, rows):
    return jnp.broadcast_to(row, (rows, row.shape[-1]))


def _resident(shape):
    nd = len(shape)
    return pl.BlockSpec(shape, lambda *_: (0,) * nd, pipeline_mode=pl.Buffered(1))


def _resident_cols(rows, width, block):
    return pl.BlockSpec((rows, width), lambda *_: (0, block), pipeline_mode=pl.Buffered(1))


def _swiglu_residual(x, g_ref, w13_ref, w2_ref):
    n = _rms(x, g_ref[...]).astype(BF16)
    gate = _dot(n, w13_ref[:, 0:D_FF])
    up = _dot(n, w13_ref[:, D_FF:2 * D_FF])
    act = (_silu(gate) * up).astype(BF16)
    return x + 0.5 * _dot(act, w2_ref[...])


def _ffn1_kernel(x_ref, g_ref, w13_ref, w2_ref, mixg_ref, h_ref, n_ref):
    h = _swiglu_residual(x_ref[...], g_ref, w13_ref, w2_ref)
    h_ref[...] = h
    n_ref[...] = _rms(h, mixg_ref[...])


def _ffn2_kernel(x_ref, g_ref, w13_ref, w2_ref, p_ref, pleg_ref, wpg_ref, wpp_ref, fing_ref, o_ref):
    h = _swiglu_residual(x_ref[...], g_ref, w13_ref, w2_ref)
    gate = _sigmoid(_dot(_rms(h, pleg_ref[...]).astype(BF16), wpg_ref[...]))
    emb = _dot(p_ref[...].astype(BF16), wpp_ref[...])
    h = h + gate * emb
    o_ref[...] = _rms(h, fing_ref[...])


def _row_spec(width):
    return pl.BlockSpec((ROW_TILE, width), lambda i: (i, 0))


def _ffn1(x, g, w13, w2, mixg):
    t = x.shape[0]
    return pl.pallas_call(
        _ffn1_kernel,
        grid=(t // ROW_TILE,),
        in_specs=[_row_spec(D_MODEL), _resident((1, D_MODEL)), _resident(w13.shape),
                  _resident(w2.shape), _resident((1, D_MODEL))],
        out_specs=[_row_spec(D_MODEL), _row_spec(D_MODEL)],
        out_shape=[jax.ShapeDtypeStruct((t, D_MODEL), F32),
                   jax.ShapeDtypeStruct((t, D_MODEL), F32)],
        compiler_params=pltpu.CompilerParams(dimension_semantics=("arbitrary",),
                                             vmem_limit_bytes=VMEM_LIMIT),
        name="ffn1",
    )(x, g, w13, w2, mixg)


def _ffn2(x, g, w13, w2, p, pleg, wpg, wpp, fing):
    t = x.shape[0]
    return pl.pallas_call(
        _ffn2_kernel,
        grid=(t // ROW_TILE,),
        in_specs=[_row_spec(D_MODEL), _resident((1, D_MODEL)), _resident(w13.shape),
                  _resident(w2.shape), _row_spec(PLE_DIM), _resident((1, D_MODEL)),
                  _resident(wpg.shape), _resident(wpp.shape), _resident((1, D_MODEL))],
        out_specs=_row_spec(D_MODEL),
        out_shape=jax.ShapeDtypeStruct((t, D_MODEL), F32),
        compiler_params=pltpu.CompilerParams(dimension_semantics=("arbitrary",),
                                             vmem_limit_bytes=VMEM_LIMIT),
        name="ffn2",
    )(x, g, w13, w2, p, pleg, wpg, wpp, fing)


def _hgrn_kernel(n_ref, w_ref, lb_ref, gn_ref, o_ref,
                 st_ref, q_ref, k_ref, gl_ref, v_ref, gate_ref, oacc_ref):
    @pl.when(pl.program_id(1) == 0)
    def _():
        st_ref[...] = jnp.zeros_like(st_ref)

    n = n_ref[...].astype(BF16)
    a = lb_ref[...]
    e = jnp.exp(a - jnp.max(a, axis=0, keepdims=True))
    lb = e[0:1] / jnp.sum(e, axis=0, keepdims=True)

    r = lax.broadcasted_iota(jnp.int32, (CHUNK, CHUNK), 0)
    c = lax.broadcasted_iota(jnp.int32, (CHUNK, CHUNK), 1)
    tril_blk = ((c <= r) & (c // HG_BLOCK == r // HG_BLOCK)).astype(BF16)
    f_raws = [_dot(n, w_ref[:, HG_KDIM + cb * COL_BLOCK:HG_KDIM + (cb + 1) * COL_BLOCK])
              for cb in range(HG_KDIM // COL_BLOCK)]
    q_ref[...] = _dot(n, w_ref[:, 0:HG_KDIM]) * (HG_DK ** -0.5)
    v_ref[...] = _dot(n, w_ref[:, 2 * HG_KDIM:2 * HG_KDIM + HG_VDIM]).astype(BF16)
    for cb, f_raw in enumerate(f_raws):
        cols = slice(cb * COL_BLOCK, (cb + 1) * COL_BLOCK)
        f = lb[:, cols] + (1.0 - lb[:, cols]) * _sigmoid(f_raw)
        k_ref[:, cols] = 1.0 - f
        logf = jnp.log(f)
        for ci in range(N_CHUNKS):
            rows = slice(ci * CHUNK, (ci + 1) * CHUNK)
            gl_ref[rows, cols] = _dot_sel(tril_blk, logf[rows])

    t_i = lax.broadcasted_iota(jnp.int32, (CHUNK, 4 * CHUNK), 0)
    c_i = lax.broadcasted_iota(jnp.int32, (CHUNK, 4 * CHUNK), 1)
    s_i, grp = c_i % CHUNK, c_i // CHUNK
    bt, bs = t_i // HG_BLOCK, s_i // HG_BLOCK
    m_all = (((grp == 0) & (bt == bs) & (s_i <= t_i))
             | ((grp == 1) & (bt % 2 == 1) & (bs == bt - 1))
             | ((grp == 2) & (bt >= 2) & (bs < 2)))
    zero_blk = jnp.zeros((CHUNK, HG_DK), BF16)
    ones_blk = jnp.ones((HG_BLOCK, HG_KDIM), F32)
    gate_cols = HG_VDIM // N_CHUNKS

    atts = {}
    prepped = []
    for ci in range(N_CHUNKS):
        rows = slice(ci * CHUNK, (ci + 1) * CHUNK)
        q, k, gl, v = q_ref[rows, :], k_ref[rows, :], gl_ref[rows, :], v_ref[rows, :]
        tot = [gl[HG_BLOCK * i + HG_BLOCK - 1:HG_BLOCK * (i + 1), :] for i in range(4)]
        e_t = [jnp.exp(t) for t in tot]
        e01, e23 = e_t[0] * e_t[1], e_t[2] * e_t[3]
        dec = e01 * e23
        tot_rows = jnp.concatenate([_bcast_rows(t, HG_BLOCK) for t in tot], axis=0)
        qd = q * jnp.exp(gl)
        ki = k * jnp.exp(-gl)
        ke = k * jnp.exp(tot_rows - gl)
        qd32 = qd * jnp.concatenate(
            [ones_blk, _bcast_rows(e_t[0], HG_BLOCK), ones_blk, _bcast_rows(e_t[2], HG_BLOCK)], axis=0)
        ke32 = ke * jnp.concatenate(
            [_bcast_rows(e_t[1], HG_BLOCK), ones_blk, _bcast_rows(e_t[3], HG_BLOCK), ones_blk], axis=0)
        qd64 = qd32 * jnp.concatenate(
            [ones_blk, ones_blk, _bcast_rows(e01, HG_BLOCK), _bcast_rows(e01, HG_BLOCK)], axis=0)
        ke64 = ke32 * jnp.concatenate(
            [_bcast_rows(e23, HG_BLOCK), _bcast_rows(e23, HG_BLOCK), ones_blk, ones_blk], axis=0)
        qd, ki, ke, qd32, ke32, qd64, ke64 = (
            t.astype(BF16) for t in (qd, ki, ke, qd32, ke32, qd64, ke64))
        prepped.append((v, dec, qd64, ke64))
        g0 = 2 * HG_KDIM + HG_VDIM + ci * gate_cols
        gate_ref[:, ci * gate_cols:(ci + 1) * gate_cols] = _silu(_dot(n, w_ref[:, g0:g0 + gate_cols]))
        for h in range(HG_HEADS):
            kl = slice(h * HG_DK, (h + 1) * HG_DK)
            near = _dot_nt(qd[:, kl], jnp.concatenate([ki[:, kl], ke[:, kl]], axis=0))
            far = _dot_nt(qd32[:, kl], ke32[:, kl])
            pairs = jnp.concatenate([near, far, jnp.zeros_like(far)], axis=1)
            atts[ci, h] = jnp.where(m_all, pairs, 0.0).astype(BF16)
    gn = gn_ref[...]
    first_head = lax.broadcasted_iota(jnp.int32, (CHUNK, 2 * HG_DV), 1) < HG_DV
    for ci in range(N_CHUNKS):
        rows = slice(ci * CHUNK, (ci + 1) * CHUNK)
        v, dec, qd64, ke64 = prepped[ci]
        for h in range(0, HG_HEADS, 2):
            kl2 = slice(h * HG_DK, (h + 2) * HG_DK)
            vl2 = slice(h * HG_DV, (h + 2) * HG_DV)
            st_a, st_b = st_ref[h], st_ref[h + 1]
            v2 = v[:, vl2]
            zero_v = jnp.zeros_like(v2)
            v_a = jnp.where(first_head, v2, zero_v)
            v_b = jnp.where(first_head, zero_v, v2)
            v_bd = jnp.concatenate([v_a, v_a, v_a, zero_v, v_b, v_b, v_b, zero_v], axis=0)
            zero_s = jnp.zeros((HG_DV, HG_DK), BF16)
            st_bd = jnp.concatenate([jnp.concatenate([st_a.astype(BF16), zero_s], axis=1),
                                     jnp.concatenate([zero_s, st_b.astype(BF16)], axis=1)], axis=0)
            att2 = jnp.concatenate([atts[ci, h], atts[ci, h + 1]], axis=1)
            oacc_ref[rows, vl2] = _dot(att2, v_bd) + _dot_nt(qd64[:, kl2], st_bd)
            incr = _dot_tn(v2, ke64[:, kl2])
            st_ref[h] = st_a * dec[:, h * HG_DK:(h + 1) * HG_DK] + incr[0:HG_DV, 0:HG_DK]
            st_ref[h + 1] = (st_b * dec[:, (h + 1) * HG_DK:(h + 2) * HG_DK]
                             + incr[HG_DV:2 * HG_DV, HG_DK:2 * HG_DK])
        for h in range(HG_HEADS):
            vl = slice(h * HG_DV, (h + 1) * HG_DV)
            o_ref[rows, vl] = (_rms(oacc_ref[rows, vl], gn[:, vl]) * gate_ref[rows, vl]).astype(BF16)


def _hgrn(n, w, hg_lb, hg_norm, batch, seq):
    tiles = seq // SEQ_TILE
    seq_spec = pl.BlockSpec((SEQ_TILE, D_MODEL), lambda b, j: (b * tiles + j, 0))
    return pl.pallas_call(
        _hgrn_kernel,
        grid=(batch, tiles),
        in_specs=[seq_spec, _resident_cols(D_MODEL, 2 * HG_KDIM + 2 * HG_VDIM, 0), _resident(hg_lb.shape),
                  _resident((1, HG_VDIM))],
        out_specs=pl.BlockSpec((SEQ_TILE, HG_VDIM), lambda b, j: (b * tiles + j, 0)),
        out_shape=jax.ShapeDtypeStruct((batch * seq, HG_VDIM), BF16),
        scratch_shapes=[
            pltpu.VMEM((HG_HEADS, HG_DV, HG_DK), F32),
            pltpu.VMEM((SEQ_TILE, HG_KDIM), F32),
            pltpu.VMEM((SEQ_TILE, HG_KDIM), F32),
            pltpu.VMEM((SEQ_TILE, HG_KDIM), F32),
            pltpu.VMEM((SEQ_TILE, HG_VDIM), BF16),
            pltpu.VMEM((SEQ_TILE, HG_VDIM), F32),
            pltpu.VMEM((SEQ_TILE, HG_VDIM), F32),
        ],
        compiler_params=pltpu.CompilerParams(dimension_semantics=("arbitrary", "arbitrary"),
                                             vmem_limit_bytes=VMEM_LIMIT),
        name="hgrn2",
    )(n, w, hg_lb, hg_norm)


_PAIRS = M_HEADS // 2
_PAIR_W = 2 * M_HEADDIM


def _ssd_kernel(n_ref, wz_ref, wxbc_ref, wdt_ref, wdtt_ref, cw_ref, cb_ref, dtb_ref, aneg_ref,
                dtbt_ref, anegt_ref, dsk_ref, gn_ref, y_ref,
                st_ref, hist_ref, edge_ref, xbc_ref, zs_ref, yacc_ref):
    @pl.when(pl.program_id(1) == 0)
    def _():
        st_ref[...] = jnp.zeros_like(st_ref)
        hist_ref[...] = jnp.zeros_like(hist_ref)

    n = n_ref[...].astype(BF16)
    r = lax.broadcasted_iota(jnp.int32, (SEQ_TILE, SEQ_TILE), 0)
    c = lax.broadcasted_iota(jnp.int32, (SEQ_TILE, SEQ_TILE), 1)
    same_chunk = c // CHUNK == r // CHUNK
    tril_chunk = ((c <= r) & same_chunk).astype(BF16)
    triu_chunk = ((r <= c) & same_chunk).astype(BF16)
    e_r = lax.broadcasted_iota(jnp.int32, (_PAIR_W, COL_BLOCK), 0)
    e_c = lax.broadcasted_iota(jnp.int32, (_PAIR_W, COL_BLOCK), 1)

    dt_c = _softplus(_dot(n, wdt_ref[...]) + dtb_ref[...])
    a_c = dt_c * aneg_ref[...]
    acs_c = jnp.concatenate([_dot_sel(tril_chunk[0:CHUNK, 0:CHUNK], a_c[ci * CHUNK:(ci + 1) * CHUNK])
                             for ci in range(N_CHUNKS)], axis=0)
    dt_parts = jnp.concatenate(_split_bf16(dt_c, 2), axis=1)
    acs_parts = jnp.concatenate(_split_bf16(acs_c, 3), axis=1)
    a_t = _softplus(_dot_nt(wdtt_ref[...], n) + dtbt_ref[...]) * anegt_ref[...]
    acs_t = _dot_sel_r(a_t, triu_chunk, 3)

    first_expand = (M_CONV_DIM - M_DINNER) // COL_BLOCK
    dt_blocks, acs_blocks = [], []
    for cb_i in range(M_CONV_DIM // COL_BLOCK):
        cols = slice(cb_i * COL_BLOCK, (cb_i + 1) * COL_BLOCK)
        x = _dot(n, wxbc_ref[:, cols])
        cw = cw_ref[:, cols]
        cb = cb_ref[:, cols]
        x1 = pltpu.roll(x, 1, 0)
        near = x * cw[3:4] + x1 * cw[2:3]
        far = x * cw[1:2] + x1 * cw[0:1]
        xbc_ref[:, cols] = _silu(near + pltpu.roll(far, 2, 0) + cb)
        edge_ref[0:8, cols] = hist_ref[:, cols]
        edge_ref[8:16, cols] = x[0:8]
        top = _bcast_rows(cb, 8)
        for kk in range(M_CONV):
            top = top + edge_ref[8 - (M_CONV - 1) + kk:16 - (M_CONV - 1) + kk, cols] * cw[kk:kk + 1]
        xbc_ref[0:8, cols] = _silu(top)
        hist_ref[:, cols] = x[SEQ_TILE - 8:SEQ_TILE]
        if cb_i >= first_expand:
            expand = ((e_c + (cb_i - first_expand) * COL_BLOCK) // M_HEADDIM == e_r).astype(BF16)
            dt_blocks.append(_dot(dt_parts, jnp.concatenate([expand] * 2, axis=0)))
            acs_blocks.append(_dot(acs_parts, jnp.concatenate([expand] * 3, axis=0)))
    dt_e = jnp.concatenate(dt_blocks, axis=1)
    acs_e = jnp.concatenate(acs_blocks, axis=1)

    lane_left = lax.broadcasted_iota(jnp.int32, (_PAIRS, _PAIR_W), 1) < CHUNK
    acs_pair = []
    for vc in range(SEQ_TILE // _PAIR_W):
        ev = acs_t[0:_PAIRS, vc * _PAIR_W:(vc + 1) * _PAIR_W]
        od = acs_t[_PAIRS:2 * _PAIRS, vc * _PAIR_W:(vc + 1) * _PAIR_W]
        acs_pair.append(jnp.where(lane_left, ev, pltpu.roll(od, CHUNK, 1)))
        acs_pair.append(jnp.where(lane_left, pltpu.roll(ev, CHUNK, 1), od))

    xdt = xbc_ref[:, 0:M_DINNER] * dt_e
    w_start = jnp.exp(acs_e)

    t_i = lax.broadcasted_iota(jnp.int32, (CHUNK, _PAIR_W), 0)
    l_i = lax.broadcasted_iota(jnp.int32, (CHUNK, _PAIR_W), 1)
    causal2 = (l_i % CHUNK) <= t_i
    left = l_i < M_HEADDIM

    row_chunk = lax.broadcasted_iota(jnp.int32, (SEQ_TILE, M_STATE), 0) // CHUNK
    acs_last = [acs_e[(ci + 1) * CHUNK - 1:(ci + 1) * CHUNK] for ci in range(N_CHUNKS)]
    acs_last_rows = jnp.concatenate([_bcast_rows(a, CHUNK) for a in acs_last], axis=0)
    xw = (xdt * jnp.exp(acs_last_rows - acs_e)).astype(BF16)
    xdt_b = xdt.astype(BF16)

    def by_chunk(v):
        zero = jnp.zeros_like(v)
        return jnp.concatenate([jnp.where(row_chunk == ci, v, zero) for ci in range(N_CHUNKS)], axis=1)

    for g in range(M_GROUPS):
        gl = slice(g * M_GROUP_W, (g + 1) * M_GROUP_W)
        b_g = xbc_ref[:, M_DINNER + g * M_STATE:M_DINNER + (g + 1) * M_STATE].astype(BF16)
        c_g = xbc_ref[:, M_DINNER + (M_GROUPS + g) * M_STATE:
                      M_DINNER + (M_GROUPS + g + 1) * M_STATE].astype(BF16)
        zs_ref[:, gl] = _silu(_dot(n, wz_ref[:, gl]))
        b_dup = jnp.concatenate([b_g[ci * CHUNK:(ci + 1) * CHUNK] for ci in range(N_CHUNKS) for _ in (0, 1)],
                                axis=0)
        cb_all = _dot_nt(c_g, b_dup)
        for ci in range(N_CHUNKS):
            rows = slice(ci * CHUNK, (ci + 1) * CHUNK)
            cb2 = cb_all[rows, ci * _PAIR_W:(ci + 1) * _PAIR_W]
            for pp in range(M_GROUP_W // _PAIR_W):
                p = g * (M_GROUP_W // _PAIR_W) + pp
                pl_ = slice(p * _PAIR_W, (p + 1) * _PAIR_W)
                seg = acs_e[rows, pl_] - acs_pair[ci][p:p + 1, :]
                decay = jnp.exp(jnp.where(causal2, seg, -jnp.inf))
                m = (cb2 * decay).astype(BF16)
                xp = xdt_b[rows, pl_]
                zero = jnp.zeros_like(xp)
                rhs = jnp.concatenate([jnp.where(left, xp, zero), jnp.where(left, zero, xp)], axis=0)
                yacc_ref[rows, pl_] = _dot(m, rhs)
        incr = _dot_tn(by_chunk(b_g), xw[:, gl])
        st = st_ref[g]
        starts = []
        for ci in range(N_CHUNKS):
            starts.append(st.astype(BF16))
            st = st * jnp.exp(acs_last[ci][:, gl]) + incr[ci * M_STATE:(ci + 1) * M_STATE]
        st_ref[g] = st
        y_inter = _dot(by_chunk(c_g), jnp.concatenate(starts, axis=0))
        y = (yacc_ref[:, gl] + y_inter * w_start[:, gl] + dsk_ref[:, gl] * xbc_ref[:, gl]) * zs_ref[:, gl]
        y_ref[:, gl] = _rms(y, gn_ref[:, gl]).astype(BF16)


def _ssd(n, w_all, wdt, wdtt, cw, cb, dtb, aneg, dtbt, anegt, dsk, gn, batch, seq):
    tiles = seq // SEQ_TILE
    seq_spec = pl.BlockSpec((SEQ_TILE, D_MODEL), lambda b, j: (b * tiles + j, 0))
    params = (wdt, wdtt, cw, cb, dtb, aneg, dtbt, anegt, dsk, gn)
    z_off, xbc_off = 2 * HG_KDIM + 2 * HG_VDIM, 2 * HG_KDIM + 2 * HG_VDIM + M_DINNER
    assert z_off % M_DINNER == 0 and xbc_off % M_CONV_DIM == 0
    return pl.pallas_call(
        _ssd_kernel,
        grid=(batch, tiles),
        in_specs=[seq_spec, _resident_cols(D_MODEL, M_DINNER, z_off // M_DINNER),
                  _resident_cols(D_MODEL, M_CONV_DIM, xbc_off // M_CONV_DIM)]
                 + [_resident(a.shape) for a in params],
        out_specs=pl.BlockSpec((SEQ_TILE, M_DINNER), lambda b, j: (b * tiles + j, 0)),
        out_shape=jax.ShapeDtypeStruct((batch * seq, M_DINNER), BF16),
        scratch_shapes=[
            pltpu.VMEM((M_GROUPS, M_STATE, M_GROUP_W), F32),
            pltpu.VMEM((8, M_CONV_DIM), F32),
            pltpu.VMEM((16, M_CONV_DIM), F32),
            pltpu.VMEM((SEQ_TILE, M_CONV_DIM), F32),
            pltpu.VMEM((SEQ_TILE, M_DINNER), F32),
            pltpu.VMEM((SEQ_TILE, M_DINNER), F32),
        ],
        compiler_params=pltpu.CompilerParams(dimension_semantics=("arbitrary", "arbitrary"),
                                             vmem_limit_bytes=VMEM_LIMIT),
        name="ssd",
    )(n, w_all, w_all, *params)


def _merge_kernel(h_ref, n_ref, ohg_ref, y_ref, wbr_ref, whg_ref, wssm_ref, wout_ref, o_ref):
    br = _sigmoid(_dot(n_ref[...].astype(BF16), wbr_ref[...]))
    mixed = (br[:, 0:D_MODEL] * _dot(ohg_ref[...], whg_ref[...])
             + br[:, D_MODEL:2 * D_MODEL] * _dot(y_ref[...], wssm_ref[...]))
    o_ref[...] = h_ref[...] + _dot(mixed.astype(BF16), wout_ref[...])


def _merge(h, n, ohg, y, wbr, whg, wssm, wout):
    t = h.shape[0]
    return pl.pallas_call(
        _merge_kernel,
        grid=(t // ROW_TILE,),
        in_specs=[_row_spec(D_MODEL), _row_spec(D_MODEL), _row_spec(HG_VDIM), _row_spec(M_DINNER),
                  _resident(wbr.shape), _resident(whg.shape), _resident(wssm.shape),
                  _resident(wout.shape)],
        out_specs=_row_spec(D_MODEL),
        out_shape=jax.ShapeDtypeStruct((t, D_MODEL), F32),
        compiler_params=pltpu.CompilerParams(dimension_semantics=("arbitrary",),
                                             vmem_limit_bytes=VMEM_LIMIT),
        name="merge",
    )(h, n, ohg, y, wbr, whg, wssm, wout)


def kernel(x, p, ffn1_norm, ffn1_w13, ffn1_w2, mix_norm, w_in, conv_w, conv_b, dt_bias, a_log,
           d_skip, ssm_norm, hg_lb, hg_norm, w_hg_out, w_ssm_out, w_out, ffn2_norm, ffn2_w13,
           ffn2_w2, ple_norm, w_ple_gate, w_ple_proj, final_norm):
    batch, seq, _ = x.shape
    assert ffn1_w13.shape[0] == 1 and hg_lb.shape[0] == 2, "single-layer block"
    assert seq % SEQ_TILE == 0 and (batch * seq) % ROW_TILE == 0
    t = batch * seq
    row = lambda v: v.reshape(1, -1).astype(F32)
    col = lambda v: v.reshape(-1, 1).astype(F32)
    expand = lambda v: jnp.repeat(v, M_HEADDIM, axis=-1)
    pad_heads = lambda v: jnp.pad(v, [(0, 0)] * (v.ndim - 1) + [(0, _PAIR_W - M_HEADS)])
    even_odd = lambda v: jnp.concatenate([v[..., 0::2], v[..., 1::2]], axis=-1)

    sizes = (HG_KDIM, HG_KDIM, HG_VDIM, HG_VDIM, M_DINNER, M_CONV_DIM, M_HEADS, 2 * D_MODEL)
    offs = [0]
    for s in sizes:
        offs.append(offs[-1] + s)
    w = w_in[0].astype(BF16)
    w_dt = w[:, offs[6]:offs[7]]
    w_br = w[:, offs[7]:offs[8]]
    a_neg = -jnp.exp(a_log[0].astype(F32))
    dtb = dt_bias[0].astype(F32)

    x2 = x.reshape(t, D_MODEL)
    h1, n = _ffn1(x2, row(ffn1_norm[0]), ffn1_w13[0].astype(BF16), ffn1_w2[0].astype(BF16),
                  row(mix_norm[0]))
    o_hg = _hgrn(n, w, hg_lb.astype(F32), row(hg_norm[0]), batch, seq)
    y = _ssd(n, w, pad_heads(w_dt), even_odd(w_dt).T,
             conv_w[0].astype(F32), row(conv_b[0]), row(pad_heads(dtb)), row(pad_heads(a_neg)),
             col(even_odd(dtb)), col(even_odd(a_neg)),
             row(expand(d_skip[0].astype(F32))), row(ssm_norm[0]), batch, seq)
    h2 = _merge(h1, n, o_hg, y, w_br, w_hg_out[0].astype(BF16), w_ssm_out[0].astype(BF16),
                w_out[0].astype(BF16))
    out = _ffn2(h2, row(ffn2_norm[0]), ffn2_w13[0].astype(BF16), ffn2_w2[0].astype(BF16),
                p[0].reshape(t, PLE_DIM), row(ple_norm[0]), w_ple_gate[0].astype(BF16),
                w_ple_proj[0].astype(BF16), row(final_norm))
    return out.reshape(batch, seq, D_MODEL)
```
